```python
import math
import jax
import jax.numpy as jnp
from jax import lax
import numpy as np

D_MODEL = 1024
BATCH = 16
SEQ = 2048
DEPTH = 2

GMLP_WIDTH = D_MODEL // 4
GMLP_GROUPS = 4
GMLP_CHUNK = 128
POOL_WIDTH = D_MODEL // 4
POOL_WINDOWS = (2, 4, 8, 16)
POOL_GROUP = POOL_WIDTH // len(POOL_WINDOWS)
HEAD_DIM = 64
ATTN_HEADS = D_MODEL // 128
ATTN_WIDTH = ATTN_HEADS * HEAD_DIM
DILATED_CONFIGS = ((128, 1), (512, 4), (2048, 16))
ATTN_BLOCK = 128
ALIBI_MAX_BIAS = 8.0
MASK_VALUE = -1e30
CONV_WIDTH = D_MODEL // 4
CONV_KERNEL = 31
N_BRANCHES = 4
_A_END = 2 * GMLP_WIDTH
_B_END = _A_END + POOL_WIDTH
_Q_END = _B_END + ATTN_WIDTH
_K_END = _Q_END + ATTN_WIDTH
_V_END = _K_END + ATTN_WIDTH
IN_WIDTH = _V_END + 2 * CONV_WIDTH
IN_SPLITS = (_A_END, _B_END, _Q_END, _K_END, _V_END)
D_FF_DENSE = ((8 * D_MODEL // 3 + 127) // 128) * 128
N_EXPERTS = 8
TOP_K = 2
D_FF_EXPERT = 7 * D_MODEL // 2
N_DENSE = (DEPTH + 1) // 2
N_MOE = DEPTH // 2
RMS_EPS = 1e-6
LN_EPS = 1e-5

kernel_name = 'hybrid_gated_mixers_moe'


def rmsnorm(x, g):
    x32 = x.astype(jnp.float32)
    y = x32 * lax.rsqrt(jnp.mean(x32 * x32, axis=-1, keepdims=True) + RMS_EPS)
    return (y * g.astype(jnp.float32)).astype(x.dtype)


def layernorm(x, g, b):
    x32 = x.astype(jnp.float32)
    mu = jnp.mean(x32, axis=-1, keepdims=True)
    var = jnp.mean(jnp.square(x32 - mu), axis=-1, keepdims=True)
    y = (x32 - mu) * lax.rsqrt(var + LN_EPS)
    return (y * g.astype(jnp.float32) + b.astype(jnp.float32)).astype(x.dtype)


def swiglu(x, w1, w3, w2):
    return (jax.nn.silu(x @ w1) * (x @ w3)) @ w2


def alibi_slopes(n_heads):
    return 2.0 ** (-ALIBI_MAX_BIAS * jnp.arange(1, n_heads + 1, dtype=jnp.float32) / n_heads)


def gmlp_spatial_gating(z, ln_g, ln_b, w_s, b_s):
    b, s, _ = z.shape
    u, v = jnp.split(z, 2, axis=-1)
    v = layernorm(v, ln_g, ln_b)
    v = v.reshape(b, s // GMLP_CHUNK, GMLP_CHUNK, GMLP_GROUPS, GMLP_WIDTH // GMLP_GROUPS)
    causal = jnp.tril(jnp.ones((GMLP_CHUNK, GMLP_CHUNK), dtype=bool))
    w = jnp.where(causal, w_s, 0.0).astype(v.dtype)
    mixed = jnp.einsum('gts,bcsgd->bctgd', w, v) + b_s.T.astype(v.dtype)[None, None, :, :, None]
    return u * mixed.reshape(b, s, GMLP_WIDTH)


def multiscale_pool(xp, w_pool, b_pool, scale):
    b, s, _ = xp.shape
    x32 = xp.astype(jnp.float32).reshape(b, s, len(POOL_WINDOWS), POOL_GROUP)
    csum = jnp.cumsum(x32, axis=1)
    pos1 = jnp.arange(1, s + 1, dtype=jnp.float32)
    outs = []
    for g, w in enumerate(POOL_WINDOWS):
        c = csum[:, :, g]
        c_lag = jnp.pad(c, ((0, 0), (w, 0), (0, 0)))[:, :s]
        count = jnp.minimum(pos1, float(w))[None, :, None]
        outs.append((c - c_lag) / count - x32[:, :, g])
    pooled = jnp.stack(outs, axis=2)
    y = jnp.einsum('bsgc,gcd->bsgd', pooled, w_pool.astype(jnp.float32)) + b_pool.astype(jnp.float32)
    return (y.reshape(b, s, POOL_WIDTH) * scale.astype(jnp.float32)).astype(xp.dtype)


def dilated_window_attention(q, k, v, window, dilation, slopes):
    b, s, h, dh = q.shape
    steps = window // dilation
    sub_len = s // dilation
    n_blk = -(-sub_len // ATTN_BLOCK)
    sub_pad = n_blk * ATTN_BLOCK

    def to_blocks(t):
        t = t.reshape(b, sub_len, dilation, h, dh).transpose(0, 2, 1, 3, 4)
        t = jnp.pad(t, ((0, 0), (0, 0), (0, sub_pad - sub_len), (0, 0), (0, 0)))
        return t.reshape(b, dilation, n_blk, ATTN_BLOCK, h, dh)

    def with_prev(t):
        prev = jnp.pad(t[:, :, :-1], ((0, 0), (0, 0), (1, 0), (0, 0), (0, 0), (0, 0)))
        return jnp.concatenate([prev, t], axis=3)

    qb = to_blocks(q)
    kk = with_prev(to_blocks(k))
    vv = with_prev(to_blocks(v))
    scores = jnp.einsum('brnqhd,brnkhd->brnqhk', qb, kk,
                        preferred_element_type=jnp.float32) / math.sqrt(dh)
    qi = jnp.arange(ATTN_BLOCK)[:, None]
    kj = jnp.arange(2 * ATTN_BLOCK)[None, :]
    dist = qi + ATTN_BLOCK - kj
    blk = jnp.arange(n_blk)[:, None, None]
    key_pos = blk * ATTN_BLOCK + kj[None] - ATTN_BLOCK
    valid = (dist >= 0)[None] & (dist <= steps)[None] & (key_pos >= 0)
    alibi = -slopes[None, :, None] * (dist * dilation).astype(jnp.float32)[:, None, :]
    scores = jnp.where(valid[:, :, None, :], scores + alibi, MASK_VALUE)
    m = jnp.max(scores, axis=-1, keepdims=True)
    p = jnp.exp(scores - m)
    den = jnp.sum(p, axis=-1)
    o = jnp.einsum('brnqhk,brnkhd->brnqhd', p, vv.astype(jnp.float32)) / den[..., None]
    lse = m[..., 0] + jnp.log(den)

    def from_blocks(t):
        t = t.reshape((b, dilation, sub_pad) + t.shape[4:])[:, :, :sub_len]
        t = jnp.moveaxis(t, 1, 2)
        return t.reshape((b, s) + t.shape[3:])

    return from_blocks(o), from_blocks(lse)


def dilated_mixture_attention(q, k, v, slopes):
    outs, lses = [], []
    for window, dilation in DILATED_CONFIGS:
        o, lse = dilated_window_attention(q, k, v, window, dilation, slopes)
        outs.append(o)
        lses.append(lse)
    alpha = jax.nn.softmax(jnp.stack(lses, axis=0), axis=0)
    o = jnp.sum(alpha[..., None] * jnp.stack(outs, axis=0), axis=0)
    return o.astype(q.dtype)


def conformer_conv(z, w_dw, b_dw, ln_g, ln_b):
    a, g = jnp.split(z, 2, axis=-1)
    y = a * jax.nn.sigmoid(g)
    y = lax.conv_general_dilated(
        y, w_dw[:, None, :].astype(y.dtype), window_strides=(1,),
        padding=((CONV_KERNEL - 1, 0),), dimension_numbers=('NWC', 'WIO', 'NWC'),
        feature_group_count=CONV_WIDTH) + b_dw.astype(y.dtype)
    y = layernorm(y, ln_g, ln_b)
    return jax.nn.silu(y)


def moe_swiglu(h, w_router, w1, w3, w2):
    b, s, d = h.shape
    t = h.reshape(b * s, d)
    logits = jnp.dot(t, w_router, preferred_element_type=jnp.float32)
    top_val, top_idx = lax.top_k(logits, TOP_K)
    top_w = jax.nn.softmax(top_val, axis=-1)
    combine = jnp.sum(jax.nn.one_hot(top_idx, N_EXPERTS, dtype=jnp.float32) * top_w[..., None], axis=1)
    out = jnp.zeros((b * s, d), jnp.float32)
    for e in range(N_EXPERTS):
        out = out + combine[:, e:e + 1] * swiglu(t, w1[e], w3[e], w2[e]).astype(jnp.float32)
    return out.reshape(b, s, d).astype(h.dtype)


def setup_inputs(seed: int = 0) -> dict:
    key = jax.random.key(seed)
    ks = iter(jax.random.split(key, 40))

    def nrm(shape, scale):
        return jax.random.normal(next(ks), shape, jnp.float32) * scale

    def gain(shape):
        return 1.0 + nrm(shape, 0.1)

    L = DEPTH
    return {
        'x': nrm((BATCH, SEQ, D_MODEL), 1.0),
        'norm_mix': gain((L, D_MODEL)),
        'w_in': nrm((L, D_MODEL, IN_WIDTH), D_MODEL ** -0.5),
        'gmlp_ln_g': gain((L, GMLP_WIDTH)),
        'gmlp_ln_b': nrm((L, GMLP_WIDTH), 0.02),
        'gmlp_w_s': nrm((L, GMLP_GROUPS, GMLP_CHUNK, GMLP_CHUNK), GMLP_CHUNK ** -0.5),
        'gmlp_b_s': gain((L, GMLP_GROUPS, GMLP_CHUNK)),
        'pool_w': nrm((L, len(POOL_WINDOWS), POOL_GROUP, POOL_GROUP), POOL_GROUP ** -0.5),
        'pool_b': nrm((L, len(POOL_WINDOWS), POOL_GROUP), 0.02),
        'pool_scale': gain((L, POOL_WIDTH)),
        'conv_w': nrm((L, CONV_KERNEL, CONV_WIDTH), CONV_KERNEL ** -0.5),
        'conv_b': nrm((L, CONV_WIDTH), 0.02),
        'conv_ln_g': gain((L, CONV_WIDTH)),
        'conv_ln_b': nrm((L, CONV_WIDTH), 0.02),
        'w_br_a': nrm((L, GMLP_WIDTH, D_MODEL), GMLP_WIDTH ** -0.5),
        'w_br_b': nrm((L, POOL_WIDTH, D_MODEL), POOL_WIDTH ** -0.5),
        'w_br_c': nrm((L, ATTN_WIDTH, D_MODEL), ATTN_WIDTH ** -0.5),
        'w_br_d': nrm((L, CONV_WIDTH, D_MODEL), CONV_WIDTH ** -0.5),
        'w_gate': nrm((L, D_MODEL, N_BRANCHES * D_MODEL), D_MODEL ** -0.5),
        'b_gate': nrm((L, N_BRANCHES * D_MODEL), 0.02),
        'w_out': nrm((L, D_MODEL, D_MODEL), D_MODEL ** -0.5),
        'norm_ffn': gain((L, D_MODEL)),
        'dense_w1': nrm((N_DENSE, D_MODEL, D_FF_DENSE), D_MODEL ** -0.5),
        'dense_w3': nrm((N_DENSE, D_MODEL, D_FF_DENSE), D_MODEL ** -0.5),
        'dense_w2': nrm((N_DENSE, D_FF_DENSE, D_MODEL), D_FF_DENSE ** -0.5),
        'moe_router': nrm((N_MOE, D_MODEL, N_EXPERTS), D_MODEL ** -0.5),
        'moe_w1': nrm((N_MOE, N_EXPERTS, D_MODEL, D_FF_EXPERT), D_MODEL ** -0.5),
        'moe_w3': nrm((N_MOE, N_EXPERTS, D_MODEL, D_FF_EXPERT), D_MODEL ** -0.5),
        'moe_w2': nrm((N_MOE, N_EXPERTS, D_FF_EXPERT, D_MODEL), D_FF_EXPERT ** -0.5),
        'norm_final': gain((D_MODEL,)),
    }


def reference(x, norm_mix, w_in, gmlp_ln_g, gmlp_ln_b, gmlp_w_s, gmlp_b_s, pool_w, pool_b, pool_scale,
              conv_w, conv_b, conv_ln_g, conv_ln_b, w_br_a, w_br_b, w_br_c, w_br_d, w_gate, b_gate, w_out,
              norm_ffn, dense_w1, dense_w3, dense_w2, moe_router, moe_w1, moe_w3, moe_w2, norm_final):
    b, s, d = x.shape
    slopes = alibi_slopes(ATTN_HEADS)
    for l in range(DEPTH):
        h = rmsnorm(x, norm_mix[l])
        z = h @ w_in[l]
        z_a, z_b, z_q, z_k, z_v, z_d = jnp.split(z, IN_SPLITS, axis=-1)
        y_a = gmlp_spatial_gating(jax.nn.gelu(z_a), gmlp_ln_g[l], gmlp_ln_b[l], gmlp_w_s[l], gmlp_b_s[l])
        y_b = multiscale_pool(z_b, pool_w[l], pool_b[l], pool_scale[l])
        q = z_q.reshape(b, s, ATTN_HEADS, HEAD_DIM)
        k = z_k.reshape(b, s, ATTN_HEADS, HEAD_DIM)
        v = z_v.reshape(b, s, ATTN_HEADS, HEAD_DIM)
        y_c = dilated_mixture_attention(q, k, v, slopes).reshape(b, s, ATTN_WIDTH)
        y_d = conformer_conv(z_d, conv_w[l], conv_b[l], conv_ln_g[l], conv_ln_b[l])
        gates = jax.nn.sigmoid(h @ w_gate[l] + b_gate[l]).reshape(b, s, N_BRANCHES, d)
        merged = (gates[:, :, 0] * (y_a @ w_br_a[l]) + gates[:, :, 1] * (y_b @ w_br_b[l])
                  + gates[:, :, 2] * (y_c @ w_br_c[l]) + gates[:, :, 3] * (y_d @ w_br_d[l]))
        x = x + (merged @ w_out[l]).astype(x.dtype)
        h = rmsnorm(x, norm_ffn[l])
        i = l // 2
        if l % 2 == 0:
            f = swiglu(h, dense_w1[i], dense_w3[i], dense_w2[i])
        else:
            f = moe_swiglu(h, moe_router[i], moe_w1[i], moe_w3[i], moe_w2[i])
        x = x + f.astype(x.dtype)
    return rmsnorm(x, norm_final)
```

```python
import functools
import math

import jax
import jax.numpy as jnp
import numpy as np
from jax import lax
from jax.experimental import pallas as pl
from jax.experimental.pallas import tpu as pltpu

F32 = jnp.float32
BF16 = jnp.bfloat16

D_MODEL = 1024
GMLP_WIDTH = 256
GMLP_GROUPS = 4
GMLP_CHUNK = 128
POOL_WIDTH = 256
POOL_WINDOWS = (2, 4, 8, 16)
POOL_GROUP = 64
HEAD_DIM = 64
ATTN_HEADS = 8
ATTN_WIDTH = 512
DILATED_CONFIGS = ((128, 1), (512, 4), (2048, 16))
ATTN_BLOCK = 128
ALIBI_MAX_BIAS = 8.0
MASK_VALUE = -1e30
CONV_WIDTH = 256
CONV_KERNEL = 31
N_EXPERTS = 8
RMS_EPS = 1e-6
LN_EPS = 1e-5

A_END = 2 * GMLP_WIDTH
B_END = A_END + POOL_WIDTH
Q_END = B_END + ATTN_WIDTH
V_END = Q_END + 2 * ATTN_WIDTH
IN_WIDTH = V_END + 2 * CONV_WIDTH

LANES = 128
VMEM_LIMIT = 56 * 1024 * 1024


def _params(semantics):
    return pltpu.CompilerParams(dimension_semantics=semantics, vmem_limit_bytes=VMEM_LIMIT)


def _rms(x, g):
    return x * lax.rsqrt(jnp.mean(x * x, axis=-1, keepdims=True) + RMS_EPS) * g


def _layernorm(x, g, b):
    mu = jnp.mean(x, axis=-1, keepdims=True)
    xc = x - mu
    var = jnp.mean(xc * xc, axis=-1, keepdims=True)
    return xc * lax.rsqrt(var + LN_EPS) * g + b


def _dot(a, b):
    return jnp.dot(a, b, preferred_element_type=F32)


def _inproj_kernel(x_ref, g_ref, w_ref, za_ref, zb_ref, qkv_ref, zd_ref):
    h = _rms(x_ref[...], g_ref[...]).astype(BF16)
    za_ref[...] = _dot(h, w_ref[:, :A_END])
    zb_ref[...] = _dot(h, w_ref[:, A_END:B_END])
    q = _dot(h, w_ref[:, B_END:Q_END]) * (1.0 / math.sqrt(HEAD_DIM))
    qkv_ref[:, :ATTN_WIDTH] = q.astype(BF16)
    qkv_ref[:, ATTN_WIDTH:] = _dot(h, w_ref[:, Q_END:V_END]).astype(BF16)
    zd_ref[...] = _dot(h, w_ref[:, V_END:])


def _inproj(x2, g, w, tm=512):
    t = x2.shape[0]
    row = lambda i: (i, 0)
    fixed = lambda i: (0, 0)
    return pl.pallas_call(
        _inproj_kernel,
        grid=(t // tm,),
        in_specs=[pl.BlockSpec((tm, D_MODEL), row),
                  pl.BlockSpec((1, D_MODEL), fixed),
                  pl.BlockSpec((D_MODEL, IN_WIDTH), fixed)],
        out_specs=[pl.BlockSpec((tm, A_END), row),
                   pl.BlockSpec((tm, POOL_WIDTH), row),
                   pl.BlockSpec((tm, 3 * ATTN_WIDTH), row),
                   pl.BlockSpec((tm, 2 * CONV_WIDTH), row)],
        out_shape=[jax.ShapeDtypeStruct((t, A_END), F32),
                   jax.ShapeDtypeStruct((t, POOL_WIDTH), F32),
                   jax.ShapeDtypeStruct((t, 3 * ATTN_WIDTH), BF16),
                   jax.ShapeDtypeStruct((t, 2 * CONV_WIDTH), F32)],
        compiler_params=_params(("parallel",)),
        name="inproj",
    )(x2, g, w)


def _gmlp_kernel(z_ref, g_ref, b_ref, w_ref, bias_ref, o_ref, *, chunks):
    gw = GMLP_WIDTH // GMLP_GROUPS
    row = lax.broadcasted_iota(jnp.int32, (GMLP_CHUNK, GMLP_GROUPS * GMLP_CHUNK), 0)
    col = lax.broadcasted_iota(jnp.int32, (GMLP_CHUNK, GMLP_GROUPS * GMLP_CHUNK), 1)
    w = w_ref[...]
    w = jnp.where((col & (GMLP_CHUNK - 1)) <= row, w, jnp.zeros_like(w))
    lane_group = lax.broadcasted_iota(jnp.int32, (GMLP_CHUNK, GMLP_WIDTH), 1) // gw
    for c in range(chunks):
        z = jax.nn.gelu(z_ref[c * GMLP_CHUNK:(c + 1) * GMLP_CHUNK, :])
        u = z[:, :GMLP_WIDTH]
        v = _layernorm(z[:, GMLP_WIDTH:], g_ref[...], b_ref[...]).astype(BF16)
        stacked = jnp.concatenate(
            [jnp.where(lane_group == g, v, jnp.zeros_like(v)) for g in range(GMLP_GROUPS)], axis=0)
        mixed = _dot(w, stacked) + bias_ref[...]
        o_ref[c * GMLP_CHUNK:(c + 1) * GMLP_CHUNK, :] = (u * mixed).astype(BF16)


def _gmlp(za, ln_g, ln_b, w_cat, bias, tm=512):
    t = za.shape[0]
    row = lambda i: (i, 0)
    fixed = lambda i: (0, 0)
    return pl.pallas_call(
        functools.partial(_gmlp_kernel, chunks=tm // GMLP_CHUNK),
        grid=(t // tm,),
        in_specs=[pl.BlockSpec((tm, A_END), row),
                  pl.BlockSpec((1, GMLP_WIDTH), fixed),
                  pl.BlockSpec((1, GMLP_WIDTH), fixed),
                  pl.BlockSpec((GMLP_CHUNK, GMLP_GROUPS * GMLP_CHUNK), fixed),
                  pl.BlockSpec((GMLP_CHUNK, GMLP_WIDTH), fixed)],
        out_specs=pl.BlockSpec((tm, GMLP_WIDTH), row),
        out_shape=jax.ShapeDtypeStruct((t, GMLP_WIDTH), BF16),
        compiler_params=_params(("parallel",)),
        name="gmlp",
    )(za, ln_g, ln_b, w_cat, bias)


def _pool_kernel(z_ref, w_ref, b_ref, s_ref, o_ref):
    x = z_ref[...]
    seq = x.shape[0]
    t = lax.broadcasted_iota(jnp.int32, x.shape, 0)
    lane = lax.broadcasted_iota(jnp.int32, x.shape, 1)

    def lagged(a, k):
        return jnp.where(t >= k, pltpu.roll(a, k, 0), 0.0)

    sums = []
    s = x
    for w in POOL_WINDOWS:
        s = s + lagged(s, w // 2)
        sums.append(s)
    group = lane // POOL_GROUP
    sel = sums[-1]
    width = jnp.full(x.shape, float(POOL_WINDOWS[-1]), F32)
    for g in range(len(POOL_WINDOWS) - 2, -1, -1):
        sel = jnp.where(group == g, sums[g], sel)
        width = jnp.where(group == g, float(POOL_WINDOWS[g]), width)
    count = jnp.minimum((t + 1).astype(F32), width)
    pooled = sel / count - x
    y = _dot(pooled.astype(BF16), w_ref[...]) + b_ref[...]
    o_ref[...] = (y * s_ref[...]).astype(BF16)
    del seq


def _pool(zb, w_bd, b, scale, seq):
    t = zb.shape[0]
    row = lambda i: (i, 0)
    fixed = lambda i: (0, 0)
    return pl.pallas_call(
        _pool_kernel,
        grid=(t // seq,),
        in_specs=[pl.BlockSpec((seq, POOL_WIDTH), row),
                  pl.BlockSpec((POOL_WIDTH, POOL_WIDTH), fixed),
                  pl.BlockSpec((1, POOL_WIDTH), fixed),
                  pl.BlockSpec((1, POOL_WIDTH), fixed)],
        out_specs=pl.BlockSpec((seq, POOL_WIDTH), row),
        out_shape=jax.ShapeDtypeStruct((t, POOL_WIDTH), BF16),
        compiler_params=_params(("parallel",)),
        name="pool",
    )(zb, w_bd, b, scale)


CONV_PAD = 32
CONV_ROWS = 256


def _conv_kernel(z_ref, w_ref, b_ref, g_ref, be_ref, o_ref, pad_ref):
    seq = o_ref.shape[0]
    pad_ref[:CONV_PAD, :] = jnp.zeros((CONV_PAD, CONV_WIDTH), F32)
    pad_ref[CONV_PAD:, :] = z_ref[:, :CONV_WIDTH] * jax.nn.sigmoid(z_ref[:, CONV_WIDTH:])
    first = CONV_PAD - (CONV_KERNEL - 1)
    for r in range(seq // CONV_ROWS):
        r0 = r * CONV_ROWS
        acc = jnp.zeros((CONV_ROWS, CONV_WIDTH), F32)
        for j in range(CONV_KERNEL):
            acc = acc + pad_ref[r0 + first + j:r0 + first + j + CONV_ROWS, :] * w_ref[j:j + 1, :]
        y = _layernorm(acc + b_ref[...], g_ref[...], be_ref[...])
        o_ref[r0:r0 + CONV_ROWS, :] = jax.nn.silu(y).astype(BF16)


def _conv(zd, w, b, ln_g, ln_b, seq):
    t = zd.shape[0]
    row = lambda i: (i, 0)
    fixed = lambda i: (0, 0)
    return pl.pallas_call(
        _conv_kernel,
        grid=(t // seq,),
        in_specs=[pl.BlockSpec((seq, 2 * CONV_WIDTH), row),
                  pl.BlockSpec((CONV_KERNEL, CONV_WIDTH), fixed),
                  pl.BlockSpec((1, CONV_WIDTH), fixed),
                  pl.BlockSpec((1, CONV_WIDTH), fixed),
                  pl.BlockSpec((1, CONV_WIDTH), fixed)],
        out_specs=pl.BlockSpec((seq, CONV_WIDTH), row),
        out_shape=jax.ShapeDtypeStruct((t, CONV_WIDTH), BF16),
        scratch_shapes=[pltpu.VMEM((seq + CONV_PAD, CONV_WIDTH), F32)],
        compiler_params=_params(("parallel",)),
        name="conv",
    )(zd, w, b, ln_g, ln_b)


def _multiplicity_table(seq):
    n_blk = seq // ATTN_BLOCK
    qi = np.arange(ATTN_BLOCK)[:, None]
    kj = np.arange(ATTN_BLOCK)[None, :]
    tables = []
    for off in range(n_blk):
        dist = off * ATTN_BLOCK + qi - kj
        mult = np.zeros_like(dist)
        for window, dilation in DILATED_CONFIGS:
            mult += ((dist >= 0) & (dist % dilation == 0) & (dist <= window)).astype(dist.dtype)
        tables.append(np.where(mult > 0, np.log(np.maximum(mult, 1)), MASK_VALUE))
    return np.stack(tables).astype(np.float32)


def _attn_kernel(slopes_ref, q_ref, k_ref, v_ref, lm_ref, o_ref):
    pair = pl.program_id(1)
    n_blk = q_ref.shape[0] // ATTN_BLOCK
    shape = (ATTN_BLOCK, LANES)
    lane = lax.broadcasted_iota(jnp.int32, shape, 1)
    qi = lax.broadcasted_iota(jnp.int32, shape, 0)
    rel = (qi - lane).astype(F32)
    first_head = lane < HEAD_DIM
    nt = (((1,), (1,)), ((), ()))

    def q_block(i, carry):
        q = q_ref[pl.ds(pl.multiple_of(i * ATTN_BLOCK, ATTN_BLOCK), ATTN_BLOCK), :]
        outs = []
        for hh in range(LANES // HEAD_DIM):
            slope = slopes_ref[pair * (LANES // HEAD_DIM) + hh]
            keep = first_head if hh == 0 else jnp.logical_not(first_head)
            qh = jnp.where(keep, q, jnp.zeros_like(q))
            slope_rel = slope * rel

            def k_block(j, state):
                m, l, acc = state
                rows = pl.ds(pl.multiple_of(j * ATTN_BLOCK, ATTN_BLOCK), ATTN_BLOCK)
                off = i - j
                s = lax.dot_general(qh, k_ref[rows, :], nt, preferred_element_type=F32)
                s = s + (lm_ref[off] - slope_rel - slope * (off * ATTN_BLOCK).astype(F32))
                m_new = jnp.maximum(m, jnp.max(s, axis=-1, keepdims=True))
                alpha = jnp.exp(m - m_new)
                p = jnp.exp(s - m_new)
                l = alpha * l + jnp.sum(p, axis=-1, keepdims=True)
                acc = alpha * acc + _dot(p.astype(BF16), v_ref[rows, :])
                return m_new, l, acc

            init = (jnp.full((ATTN_BLOCK, 1), -1e38, F32), jnp.zeros((ATTN_BLOCK, 1), F32),
                    jnp.zeros(shape, F32))
            _, l, acc = lax.fori_loop(0, i + 1, k_block, init)
            outs.append(acc / l)
        o = jnp.where(first_head, outs[0], outs[1])
        o_ref[pl.ds(pl.multiple_of(i * ATTN_BLOCK, ATTN_BLOCK), ATTN_BLOCK), :] = o.astype(BF16)
        return carry

    lax.fori_loop(0, n_blk, q_block, 0)


def _attention(qkv, slopes, lm, seq):
    t = qkv.shape[0]
    pairs = ATTN_WIDTH // LANES
    n_blk = seq // ATTN_BLOCK
    return pl.pallas_call(
        _attn_kernel,
        grid=(t // seq, pairs),
        in_specs=[pl.BlockSpec(memory_space=pltpu.SMEM),
                  pl.BlockSpec((seq, LANES), lambda b, p: (b, p)),
                  pl.BlockSpec((seq, LANES), lambda b, p: (b, pairs + p)),
                  pl.BlockSpec((seq, LANES), lambda b, p: (b, 2 * pairs + p)),
                  pl.BlockSpec((n_blk, ATTN_BLOCK, ATTN_BLOCK), lambda b, p: (0, 0, 0))],
        out_specs=pl.BlockSpec((seq, LANES), lambda b, p: (b, p)),
        out_shape=jax.ShapeDtypeStruct((t, ATTN_WIDTH), BF16),
        compiler_params=_params(("parallel", "parallel")),
        name="attention",
    )(slopes, qkv, qkv, qkv, lm)


def _merge_kernel(x_ref, g_ref, ya_ref, yb_ref, yc_ref, yd_ref, wg_ref, bg_ref,
                  wa_ref, wb_ref, wc_ref, wd_ref, wo_ref, o_ref):
    x = x_ref[...]
    h = _rms(x, g_ref[...]).astype(BF16)
    merged = None
    for i, (y_ref, w_ref) in enumerate(((ya_ref, wa_ref), (yb_ref, wb_ref), (yc_ref, wc_ref), (yd_ref, wd_ref))):
        cols = slice(i * D_MODEL, (i + 1) * D_MODEL)
        gate = jax.nn.sigmoid(_dot(h, wg_ref[:, cols]) + bg_ref[:, cols])
        term = gate * _dot(y_ref[...], w_ref[...])
        merged = term if merged is None else merged + term
    o_ref[...] = x + _dot(merged.astype(BF16), wo_ref[...])


def _merge(x2, g, ya, yb, yc, yd, wg, bg, wa, wb, wc, wd, wo, tm=512):
    t = x2.shape[0]
    row = lambda i: (i, 0)
    fixed = lambda i: (0, 0)
    full = lambda a: pl.BlockSpec(a.shape, fixed)
    return pl.pallas_call(
        _merge_kernel,
        grid=(t // tm,),
        in_specs=[pl.BlockSpec((tm, D_MODEL), row), full(g),
                  pl.BlockSpec((tm, GMLP_WIDTH), row), pl.BlockSpec((tm, POOL_WIDTH), row),
                  pl.BlockSpec((tm, ATTN_WIDTH), row), pl.BlockSpec((tm, CONV_WIDTH), row),
                  full(wg), full(bg), full(wa), full(wb), full(wc), full(wd), full(wo)],
        out_specs=pl.BlockSpec((tm, D_MODEL), row),
        out_shape=jax.ShapeDtypeStruct((t, D_MODEL), F32),
        compiler_params=_params(("parallel",)),
        name="merge",
    )(x2, g, ya, yb, yc, yd, wg, bg, wa, wb, wc, wd, wo)


def _dense_ffn_kernel(x_ref, g_ref, w1_ref, w3_ref, w2_ref, o_ref, h_ref, acc_ref):
    f = pl.program_id(1)

    @pl.when(f == 0)
    def _():
        h_ref[...] = _rms(x_ref[...], g_ref[...]).astype(BF16)
        acc_ref[...] = jnp.zeros_like(acc_ref)

    h = h_ref[...]
    act = jax.nn.silu(_dot(h, w1_ref[...])) * _dot(h, w3_ref[...])
    acc_ref[...] += _dot(act.astype(BF16), w2_ref[...])

    @pl.when(f == pl.num_programs(1) - 1)
    def _():
        o_ref[...] = x_ref[...] + acc_ref[...]


def _dense_ffn(x2, g, w1, w3, w2, tm=1024, tf=256):
    t = x2.shape[0]
    d_ff = w1.shape[1]
    return pl.pallas_call(
        _dense_ffn_kernel,
        grid=(t // tm, d_ff // tf),
        in_specs=[pl.BlockSpec((tm, D_MODEL), lambda i, f: (i, 0)),
                  pl.BlockSpec((1, D_MODEL), lambda i, f: (0, 0)),
                  pl.BlockSpec((D_MODEL, tf), lambda i, f: (0, f)),
                  pl.BlockSpec((D_MODEL, tf), lambda i, f: (0, f)),
                  pl.BlockSpec((tf, D_MODEL), lambda i, f: (f, 0))],
        out_specs=pl.BlockSpec((tm, D_MODEL), lambda i, f: (i, 0)),
        out_shape=jax.ShapeDtypeStruct((t, D_MODEL), F32),
        scratch_shapes=[pltpu.VMEM((tm, D_MODEL), BF16), pltpu.VMEM((tm, D_MODEL), F32)],
        compiler_params=_params(("parallel", "arbitrary")),
        name="dense_ffn",
    )(x2, g, w1, w3, w2)


def _route(logits):
    lane = lax.broadcasted_iota(jnp.int32, logits.shape, 1)
    neg = jnp.float32(-jnp.inf)
    logits = jnp.where(lane < N_EXPERTS, logits, neg)
    m1 = jnp.max(logits, axis=-1, keepdims=True)
    i1 = jnp.min(jnp.where(logits == m1, lane, LANES), axis=-1, keepdims=True)
    rest = jnp.where(lane == i1, neg, logits)
    m2 = jnp.max(rest, axis=-1, keepdims=True)
    i2 = jnp.min(jnp.where(rest == m2, lane, LANES), axis=-1, keepdims=True)
    e2 = jnp.exp(m2 - m1)
    den = 1.0 + e2
    return jnp.where(lane == i1, 1.0 / den, 0.0) + jnp.where(lane == i2, e2 / den, 0.0)


def _moe_kernel(x_ref, g_ref, wr_ref, w1_ref, w3_ref, w2_ref, gf_ref, o_ref, h_ref, acc_ref, comb_ref):
    e = pl.program_id(1)
    f = pl.program_id(2)

    @pl.when((e == 0) & (f == 0))
    def _():
        h = _rms(x_ref[...], g_ref[...])
        h_ref[...] = h.astype(BF16)
        logits = jnp.dot(h, wr_ref[...], preferred_element_type=F32, precision=lax.Precision.HIGHEST)
        comb_ref[...] = _route(logits)
        acc_ref[...] = jnp.zeros_like(acc_ref)

    h = h_ref[...]
    lane = lax.broadcasted_iota(jnp.int32, comb_ref.shape, 1)
    weight = jnp.sum(jnp.where(lane == e, comb_ref[...], 0.0), axis=-1, keepdims=True)
    act = jax.nn.silu(_dot(h, w1_ref[...])) * _dot(h, w3_ref[...])
    acc_ref[...] += weight * _dot(act.astype(BF16), w2_ref[...])

    @pl.when((e == pl.num_programs(1) - 1) & (f == pl.num_programs(2) - 1))
    def _():
        o_ref[...] = _rms(x_ref[...] + acc_ref[...], gf_ref[...])


def _moe_ffn(x2, g, w_router, w1, w3, w2, g_final, tm=1024, tf=512):
    t = x2.shape[0]
    n_exp, _, d_ff = w1.shape
    return pl.pallas_call(
        _moe_kernel,
        grid=(t // tm, n_exp, d_ff // tf),
        in_specs=[pl.BlockSpec((tm, D_MODEL), lambda i, e, f: (i, 0)),
                  pl.BlockSpec((1, D_MODEL), lambda i, e, f: (0, 0)),
                  pl.BlockSpec((D_MODEL, LANES), lambda i, e, f: (0, 0)),
                  pl.BlockSpec((None, D_MODEL, tf), lambda i, e, f: (e, 0, f)),
                  pl.BlockSpec((None, D_MODEL, tf), lambda i, e, f: (e, 0, f)),
                  pl.BlockSpec((None, tf, D_MODEL), lambda i, e, f: (e, f, 0)),
                  pl.BlockSpec((1, D_MODEL), lambda i, e, f: (0, 0))],
        out_specs=pl.BlockSpec((tm, D_MODEL), lambda i, e, f: (i, 0)),
        out_shape=jax.ShapeDtypeStruct((t, D_MODEL), F32),
        scratch_shapes=[pltpu.VMEM((tm, D_MODEL), BF16), pltpu.VMEM((tm, D_MODEL), F32),
                        pltpu.VMEM((tm, LANES), F32)],
        compiler_params=_params(("parallel", "arbitrary", "arbitrary")),
        name="moe_ffn",
    )(x2, g, w_router, w1, w3, w2, g_final)


def kernel(x, norm_mix, w_in, gmlp_ln_g, gmlp_ln_b, gmlp_w_s, gmlp_b_s, pool_w, pool_b, pool_scale, conv_w, conv_b, conv_ln_g, conv_ln_b, w_br_a, w_br_b, w_br_c, w_br_d, w_gate, b_gate, w_out, norm_ffn, dense_w1, dense_w3, dense_w2, moe_router, moe_w1, moe_w3, moe_w2, norm_final):
    b, s, d = x.shape
    depth = norm_mix.shape[0]
    assert depth == 2 and d == D_MODEL, "kernel is written for the two-layer (dense, MoE) trunk"
    x2 = x.reshape(b * s, d)
    slopes = 2.0 ** (-ALIBI_MAX_BIAS * jnp.arange(1, ATTN_HEADS + 1, dtype=F32) / ATTN_HEADS)
    lm = jnp.asarray(_multiplicity_table(s))
    row = lambda a: a.reshape(1, -1)

    for l in range(depth):
        za, zb, qkv, zd = _inproj(x2, row(norm_mix[l]), w_in[l].astype(BF16))

        w_cat = gmlp_w_s[l].transpose(1, 0, 2).reshape(GMLP_CHUNK, GMLP_GROUPS * GMLP_CHUNK).astype(BF16)
        bias_a = jnp.repeat(gmlp_b_s[l].T, GMLP_WIDTH // GMLP_GROUPS, axis=1)
        ya = _gmlp(za, row(gmlp_ln_g[l]), row(gmlp_ln_b[l]), w_cat, bias_a)

        w_bd = jax.scipy.linalg.block_diag(*[pool_w[l, g] for g in range(len(POOL_WINDOWS))]).astype(BF16)
        yb = _pool(zb, w_bd, row(pool_b[l]), row(pool_scale[l]), s)

        yc = _attention(qkv, slopes, lm, s)
        yd = _conv(zd, conv_w[l], row(conv_b[l]), row(conv_ln_g[l]), row(conv_ln_b[l]), s)

        x2 = _merge(x2, row(norm_mix[l]), ya, yb, yc, yd, w_gate[l].astype(BF16), row(b_gate[l]),
                    w_br_a[l].astype(BF16), w_br_b[l].astype(BF16), w_br_c[l].astype(BF16),
                    w_br_d[l].astype(BF16), w_out[l].astype(BF16))

        i = l // 2
        if l % 2 == 0:
            x2 = _dense_ffn(x2, row(norm_ffn[l]), dense_w1[i].astype(BF16), dense_w3[i].astype(BF16),
                            dense_w2[i].astype(BF16))
        else:
            w_router = jnp.pad(moe_router[i], ((0, 0), (0, LANES - N_EXPERTS)))
            x2 = _moe_ffn(x2, row(norm_ffn[l]), w_router, moe_w1[i].astype(BF16), moe_w3[i].astype(BF16),
                          moe_w2[i].astype(BF16), row(norm_final))
    return x2.reshape(b, s, d)
```

```python
import functools
import math

import jax
import jax.numpy as jnp
import numpy as np
from jax import lax
from jax.experimental import pallas as pl
from jax.experimental.pallas import tpu as pltpu

F32 = jnp.float32
BF16 = jnp.bfloat16

D_MODEL = 1024
GMLP_WIDTH = 256
GMLP_GROUPS = 4
GMLP_CHUNK = 128
POOL_WIDTH = 256
POOL_WINDOWS = (2, 4, 8, 16)
POOL_GROUP = 64
HEAD_DIM = 64
ATTN_HEADS = 8
ATTN_WIDTH = 512
DILATED_CONFIGS = ((128, 1), (512, 4), (2048, 16))
ATTN_BLOCK = 128
ALIBI_MAX_BIAS = 8.0
MASK_VALUE = -1e30
CONV_WIDTH = 256
CONV_KERNEL = 31
N_EXPERTS = 8
RMS_EPS = 1e-6
LN_EPS = 1e-5

A_END = 2 * GMLP_WIDTH
B_END = A_END + POOL_WIDTH
Q_END = B_END + ATTN_WIDTH
V_END = Q_END + 2 * ATTN_WIDTH
IN_WIDTH = V_END + 2 * CONV_WIDTH

LANES = 128
VMEM_LIMIT = 56 * 1024 * 1024


def _params(semantics):
    return pltpu.CompilerParams(dimension_semantics=semantics, vmem_limit_bytes=VMEM_LIMIT)


def _rms(x, g):
    return x * lax.rsqrt(jnp.mean(x * x, axis=-1, keepdims=True) + RMS_EPS) * g


def _layernorm(x, g, b):
    mu = jnp.mean(x, axis=-1, keepdims=True)
    xc = x - mu
    var = jnp.mean(xc * xc, axis=-1, keepdims=True)
    return xc * lax.rsqrt(var + LN_EPS) * g + b


def _dot(a, b):
    return jnp.dot(a, b, preferred_element_type=F32)


def _inproj_kernel(x_ref, g_ref, w_ref, za_ref, zb_ref, qkv_ref, zd_ref):
    h = _rms(x_ref[...], g_ref[...]).astype(BF16)
    za_ref[...] = _dot(h, w_ref[:, :A_END])
    zb_ref[...] = _dot(h, w_ref[:, A_END:B_END])
    q = _dot(h, w_ref[:, B_END:Q_END]) * (1.0 / math.sqrt(HEAD_DIM))
    qkv_ref[:, :ATTN_WIDTH] = q.astype(BF16)
    qkv_ref[:, ATTN_WIDTH:] = _dot(h, w_ref[:, Q_END:V_END]).astype(BF16)
    zd_ref[...] = _dot(h, w_ref[:, V_END:])


def _inproj(x2, g, w, tm=512):
    t = x2.shape[0]
    row = lambda i: (i, 0)
    fixed = lambda i: (0, 0)
    return pl.pallas_call(
        _inproj_kernel,
        grid=(t // tm,),
        in_specs=[pl.BlockSpec((tm, D_MODEL), row),
                  pl.BlockSpec((1, D_MODEL), fixed),
                  pl.BlockSpec((D_MODEL, IN_WIDTH), fixed)],
        out_specs=[pl.BlockSpec((tm, A_END), row),
                   pl.BlockSpec((tm, POOL_WIDTH), row),
                   pl.BlockSpec((tm, 3 * ATTN_WIDTH), row),
                   pl.BlockSpec((tm, 2 * CONV_WIDTH), row)],
        out_shape=[jax.ShapeDtypeStruct((t, A_END), F32),
                   jax.ShapeDtypeStruct((t, POOL_WIDTH), F32),
                   jax.ShapeDtypeStruct((t, 3 * ATTN_WIDTH), BF16),
                   jax.ShapeDtypeStruct((t, 2 * CONV_WIDTH), F32)],
        compiler_params=_params(("parallel",)),
        name="inproj",
    )(x2, g, w)


def _gmlp_kernel(z_ref, g_ref, b_ref, w_ref, bias_ref, o_ref, *, chunks):
    gw = GMLP_WIDTH // GMLP_GROUPS
    row = lax.broadcasted_iota(jnp.int32, (GMLP_CHUNK, GMLP_GROUPS * GMLP_CHUNK), 0)
    col = lax.broadcasted_iota(jnp.int32, (GMLP_CHUNK, GMLP_GROUPS * GMLP_CHUNK), 1)
    w = w_ref[...]
    w = jnp.where((col & (GMLP_CHUNK - 1)) <= row, w, jnp.zeros_like(w))
    lane_group = lax.broadcasted_iota(jnp.int32, (GMLP_CHUNK, GMLP_WIDTH), 1) // gw
    for c in range(chunks):
        z = jax.nn.gelu(z_ref[c * GMLP_CHUNK:(c + 1) * GMLP_CHUNK, :])
        u = z[:, :GMLP_WIDTH]
        v = _layernorm(z[:, GMLP_WIDTH:], g_ref[...], b_ref[...]).astype(BF16)
        stacked = jnp.concatenate(
            [jnp.where(lane_group == g, v, jnp.zeros_like(v)) for g in range(GMLP_GROUPS)], axis=0)
        mixed = _dot(w, stacked) + bias_ref[...]
        o_ref[c * GMLP_CHUNK:(c + 1) * GMLP_CHUNK, :] = (u * mixed).astype(BF16)


def _gmlp(za, ln_g, ln_b, w_cat, bias, tm=512):
    t = za.shape[0]
    row = lambda i: (i, 0)
    fixed = lambda i: (0, 0)
    return pl.pallas_call(
        functools.partial(_gmlp_kernel, chunks=tm // GMLP_CHUNK),
        grid=(t // tm,),
        in_specs=[pl.BlockSpec((tm, A_END), row),
                  pl.BlockSpec((1, GMLP_WIDTH), fixed),
                  pl.BlockSpec((1, GMLP_WIDTH), fixed),
                  pl.BlockSpec((GMLP_CHUNK, GMLP_GROUPS * GMLP_CHUNK), fixed),
                  pl.BlockSpec((GMLP_CHUNK, GMLP_WIDTH), fixed)],
        out_specs=pl.BlockSpec((tm, GMLP_WIDTH), row),
        out_shape=jax.ShapeDtypeStruct((t, GMLP_WIDTH), BF16),
        compiler_params=_params(("parallel",)),
        name="gmlp",
    )(za, ln_g, ln_b, w_cat, bias)


def _pool_kernel(z_ref, w_ref, b_ref, s_ref, o_ref):
    x = z_ref[...]
    seq = x.shape[0]
    t = lax.broadcasted_iota(jnp.int32, x.shape, 0)
    lane = lax.broadcasted_iota(jnp.int32, x.shape, 1)

    def lagged(a, k):
        return jnp.where(t >= k, pltpu.roll(a, k, 0), 0.0)

    sums = []
    s = x
    for w in POOL_WINDOWS:
        s = s + lagged(s, w // 2)
        sums.append(s)
    group = lane // POOL_GROUP
    sel = sums[-1]
    width = jnp.full(x.shape, float(POOL_WINDOWS[-1]), F32)
    for g in range(len(POOL_WINDOWS) - 2, -1, -1):
        sel = jnp.where(group == g, sums[g], sel)
        width = jnp.where(group == g, float(POOL_WINDOWS[g]), width)
    count = jnp.minimum((t + 1).astype(F32), width)
    pooled = sel / count - x
    y = _dot(pooled.astype(BF16), w_ref[...]) + b_ref[...]
    o_ref[...] = (y * s_ref[...]).astype(BF16)
    del seq


def _pool(zb, w_bd, b, scale, seq):
    t = zb.shape[0]
    row = lambda i: (i, 0)
    fixed = lambda i: (0, 0)
    return pl.pallas_call(
        _pool_kernel,
        grid=(t // seq,),
        in_specs=[pl.BlockSpec((seq, POOL_WIDTH), row),
                  pl.BlockSpec((POOL_WIDTH, POOL_WIDTH), fixed),
                  pl.BlockSpec((1, POOL_WIDTH), fixed),
                  pl.BlockSpec((1, POOL_WIDTH), fixed)],
        out_specs=pl.BlockSpec((seq, POOL_WIDTH), row),
        out_shape=jax.ShapeDtypeStruct((t, POOL_WIDTH), BF16),
        compiler_params=_params(("parallel",)),
        name="pool",
    )(zb, w_bd, b, scale)


CONV_PAD = 32
CONV_ROWS = 256


def _conv_kernel(z_ref, w_ref, b_ref, g_ref, be_ref, o_ref, pad_ref):
    seq = o_ref.shape[0]
    pad_ref[:CONV_PAD, :] = jnp.zeros((CONV_PAD, CONV_WIDTH), F32)
    pad_ref[CONV_PAD:, :] = z_ref[:, :CONV_WIDTH] * jax.nn.sigmoid(z_ref[:, CONV_WIDTH:])
    first = CONV_PAD - (CONV_KERNEL - 1)
    for r in range(seq // CONV_ROWS):
        r0 = r * CONV_ROWS
        acc = jnp.zeros((CONV_ROWS, CONV_WIDTH), F32)
        for j in range(CONV_KERNEL):
            acc = acc + pad_ref[r0 + first + j:r0 + first + j + CONV_ROWS, :] * w_ref[j:j + 1, :]
        y = _layernorm(acc + b_ref[...], g_ref[...], be_ref[...])
        o_ref[r0:r0 + CONV_ROWS, :] = jax.nn.silu(y).astype(BF16)


def _conv(zd, w, b, ln_g, ln_b, seq):
    t = zd.shape[0]
    row = lambda i: (i, 0)
    fixed = lambda i: (0, 0)
    return pl.pallas_call(
        _conv_kernel,
        grid=(t // seq,),
        in_specs=[pl.BlockSpec((seq, 2 * CONV_WIDTH), row),
                  pl.BlockSpec((CONV_KERNEL, CONV_WIDTH), fixed),
                  pl.BlockSpec((1, CONV_WIDTH), fixed),
                  pl.BlockSpec((1, CONV_WIDTH), fixed),
                  pl.BlockSpec((1, CONV_WIDTH), fixed)],
        out_specs=pl.BlockSpec((seq, CONV_WIDTH), row),
        out_shape=jax.ShapeDtypeStruct((t, CONV_WIDTH), BF16),
        scratch_shapes=[pltpu.VMEM((seq + CONV_PAD, CONV_WIDTH), F32)],
        compiler_params=_params(("parallel",)),
        name="conv",
    )(zd, w, b, ln_g, ln_b)


def _attention_bias(seq):
    qi = jnp.arange(ATTN_BLOCK, dtype=jnp.int32)[:, None]
    x = jnp.arange(seq, dtype=jnp.int32)[None, :]
    dist = seq - ATTN_BLOCK + qi - x
    mult = jnp.zeros(dist.shape, F32)
    for window, dilation in DILATED_CONFIGS:
        mult = mult + ((dist >= 0) & (dist % dilation == 0) & (dist <= window)).astype(F32)
    slopes = 2.0 ** (-ALIBI_MAX_BIAS * jnp.arange(1, ATTN_HEADS + 1, dtype=F32) / ATTN_HEADS)
    alibi = -slopes[:, None, None] * dist.astype(F32)[None]
    return jnp.where(mult > 0, jnp.log(jnp.maximum(mult, 1.0)) + alibi, MASK_VALUE)


def _attn_kernel(q_ref, k_ref, v_ref, bias_ref, o_ref):
    seq = q_ref.shape[0]
    n_blk = seq // ATTN_BLOCK
    lane = lax.broadcasted_iota(jnp.int32, (ATTN_BLOCK, LANES), 1)
    first_head = lane < HEAD_DIM
    nt = (((1,), (1,)), ((), ()))
    for i in range(n_blk):
        rows = slice(i * ATTN_BLOCK, (i + 1) * ATTN_BLOCK)
        n_keys = (i + 1) * ATTN_BLOCK
        q = q_ref[rows, :]
        outs = []
        for hh in range(LANES // HEAD_DIM):
            keep = first_head if hh == 0 else jnp.logical_not(first_head)
            qh = jnp.where(keep, q, jnp.zeros_like(q))
            s = lax.dot_general(qh, k_ref[:n_keys, :], nt, preferred_element_type=F32)
            s = s + bias_ref[hh, :, seq - n_keys:]
            p = jnp.exp(s - jnp.max(s, axis=-1, keepdims=True))
            den = jnp.sum(p, axis=-1, keepdims=True)
            outs.append(_dot(p.astype(BF16), v_ref[:n_keys, :]) / den)
        o_ref[rows, :] = jnp.where(first_head, outs[0], outs[1]).astype(BF16)


def _attention(qkv, bias, seq):
    t = qkv.shape[0]
    pairs = ATTN_WIDTH // LANES
    heads_per_pair = LANES // HEAD_DIM
    return pl.pallas_call(
        _attn_kernel,
        grid=(pairs, t // seq),
        in_specs=[pl.BlockSpec((seq, LANES), lambda p, b: (b, p)),
                  pl.BlockSpec((seq, LANES), lambda p, b: (b, pairs + p)),
                  pl.BlockSpec((seq, LANES), lambda p, b: (b, 2 * pairs + p)),
                  pl.BlockSpec((heads_per_pair, ATTN_BLOCK, seq), lambda p, b: (p, 0, 0))],
        out_specs=pl.BlockSpec((seq, LANES), lambda p, b: (b, p)),
        out_shape=jax.ShapeDtypeStruct((t, ATTN_WIDTH), BF16),
        compiler_params=_params(("parallel", "parallel")),
        name="attention",
    )(qkv, qkv, qkv, bias)


def _merge_kernel(x_ref, g_ref, ya_ref, yb_ref, yc_ref, yd_ref, wg_ref, bg_ref,
                  wa_ref, wb_ref, wc_ref, wd_ref, wo_ref, o_ref):
    x = x_ref[...]
    h = _rms(x, g_ref[...]).astype(BF16)
    merged = None
    for i, (y_ref, w_ref) in enumerate(((ya_ref, wa_ref), (yb_ref, wb_ref), (yc_ref, wc_ref), (yd_ref, wd_ref))):
        cols = slice(i * D_MODEL, (i + 1) * D_MODEL)
        gate = jax.nn.sigmoid(_dot(h, wg_ref[:, cols]) + bg_ref[:, cols])
        term = gate * _dot(y_ref[...], w_ref[...])
        merged = term if merged is None else merged + term
    o_ref[...] = x + _dot(merged.astype(BF16), wo_ref[...])


def _merge(x2, g, ya, yb, yc, yd, wg, bg, wa, wb, wc, wd, wo, tm=512):
    t = x2.shape[0]
    row = lambda i: (i, 0)
    fixed = lambda i: (0, 0)
    full = lambda a: pl.BlockSpec(a.shape, fixed)
    return pl.pallas_call(
        _merge_kernel,
        grid=(t // tm,),
        in_specs=[pl.BlockSpec((tm, D_MODEL), row), full(g),
                  pl.BlockSpec((tm, GMLP_WIDTH), row), pl.BlockSpec((tm, POOL_WIDTH), row),
                  pl.BlockSpec((tm, ATTN_WIDTH), row), pl.BlockSpec((tm, CONV_WIDTH), row),
                  full(wg), full(bg), full(wa), full(wb), full(wc), full(wd), full(wo)],
        out_specs=pl.BlockSpec((tm, D_MODEL), row),
        out_shape=jax.ShapeDtypeStruct((t, D_MODEL), F32),
        compiler_params=_params(("parallel",)),
        name="merge",
    )(x2, g, ya, yb, yc, yd, wg, bg, wa, wb, wc, wd, wo)


def _dense_ffn_kernel(x_ref, g_ref, w1_ref, w3_ref, w2_ref, o_ref, h_ref, acc_ref):
    f = pl.program_id(1)

    @pl.when(f == 0)
    def _():
        h_ref[...] = _rms(x_ref[...], g_ref[...]).astype(BF16)
        acc_ref[...] = jnp.zeros_like(acc_ref)

    h = h_ref[...]
    act = jax.nn.silu(_dot(h, w1_ref[...])) * _dot(h, w3_ref[...])
    acc_ref[...] += _dot(act.astype(BF16), w2_ref[...])

    @pl.when(f == pl.num_programs(1) - 1)
    def _():
        o_ref[...] = x_ref[...] + acc_ref[...]


def _dense_ffn(x2, g, w1, w3, w2, tm=1024, tf=256):
    t = x2.shape[0]
    d_ff = w1.shape[1]
    return pl.pallas_call(
        _dense_ffn_kernel,
        grid=(t // tm, d_ff // tf),
        in_specs=[pl.BlockSpec((tm, D_MODEL), lambda i, f: (i, 0)),
                  pl.BlockSpec((1, D_MODEL), lambda i, f: (0, 0)),
                  pl.BlockSpec((D_MODEL, tf), lambda i, f: (0, f)),
                  pl.BlockSpec((D_MODEL, tf), lambda i, f: (0, f)),
                  pl.BlockSpec((tf, D_MODEL), lambda i, f: (f, 0))],
        out_specs=pl.BlockSpec((tm, D_MODEL), lambda i, f: (i, 0)),
        out_shape=jax.ShapeDtypeStruct((t, D_MODEL), F32),
        scratch_shapes=[pltpu.VMEM((tm, D_MODEL), BF16), pltpu.VMEM((tm, D_MODEL), F32)],
        compiler_params=_params(("parallel", "arbitrary")),
        name="dense_ffn",
    )(x2, g, w1, w3, w2)


TOP_K = 2


def _router_kernel(x_ref, g_ref, wr_ref, o_ref):
    h = _rms(x_ref[...], g_ref[...])
    logits = jnp.dot(h, wr_ref[...], preferred_element_type=F32, precision=lax.Precision.HIGHEST)
    lane = lax.broadcasted_iota(jnp.int32, logits.shape, 1)
    neg = jnp.float32(-jnp.inf)
    logits = jnp.where(lane < N_EXPERTS, logits, neg)
    m1 = jnp.max(logits, axis=-1, keepdims=True)
    i1 = jnp.min(jnp.where(logits == m1, lane, LANES), axis=-1, keepdims=True)
    rest = jnp.where(lane == i1, neg, logits)
    m2 = jnp.max(rest, axis=-1, keepdims=True)
    i2 = jnp.min(jnp.where(rest == m2, lane, LANES), axis=-1, keepdims=True)
    e2 = jnp.exp(m2 - m1)
    den = 1.0 + e2
    o_ref[...] = jnp.where(lane == 0, i1.astype(F32),
                           jnp.where(lane == 1, i2.astype(F32),
                                     jnp.where(lane == 2, 1.0 / den, jnp.where(lane == 3, e2 / den, 0.0))))


def _router(x2, g, w_router, tm=1024):
    t = x2.shape[0]
    return pl.pallas_call(
        _router_kernel,
        grid=(t // tm,),
        in_specs=[pl.BlockSpec((tm, D_MODEL), lambda i: (i, 0)),
                  pl.BlockSpec((1, D_MODEL), lambda i: (0, 0)),
                  pl.BlockSpec((D_MODEL, LANES), lambda i: (0, 0))],
        out_specs=pl.BlockSpec((tm, LANES), lambda i: (i, 0)),
        out_shape=jax.ShapeDtypeStruct((t, LANES), F32),
        compiler_params=_params(("parallel",)),
        name="router",
    )(x2, g, w_router)


MOE_ROWS = 768
MOE_FF_TILE = 512
MOE_DMA_STEPS = 6
MOE_DMA_CHUNK = MOE_ROWS // MOE_DMA_STEPS


def _moe_plan(routed, n_tiles):
    t = routed.shape[0]
    n = TOP_K * t
    expert = routed[:, :TOP_K].astype(jnp.int32).reshape(n)
    weight = routed[:, TOP_K:2 * TOP_K].reshape(n)
    onehot = (expert[:, None] == jnp.arange(N_EXPERTS, dtype=jnp.int32)[None, :]).astype(jnp.int32)
    running = jnp.cumsum(onehot, axis=0)
    counts = running[-1]
    rank = jnp.sum(onehot * running, axis=1) - 1
    tiles_per = (counts + MOE_ROWS - 1) // MOE_ROWS
    tile_start = jnp.cumsum(tiles_per) - tiles_per
    slot = jnp.sum(onehot * (tile_start * MOE_ROWS)[None, :], axis=1) + rank
    n_slots = n_tiles * MOE_ROWS
    assign = jnp.full((n_slots,), -1, jnp.int32).at[slot].set(
        jnp.arange(n, dtype=jnp.int32), unique_indices=True)
    used = assign >= 0
    safe = jnp.maximum(assign, 0)
    src = jnp.where(used, safe // TOP_K, 0)
    dump = n + MOE_ROWS - 1 + jnp.cumsum(jnp.logical_not(used).astype(jnp.int32))
    dest = jnp.where(used, assign, dump)
    w_slot = jnp.where(used, weight[safe], 0.0)
    tile = jnp.arange(n_tiles, dtype=jnp.int32)
    tile_expert = jnp.sum((tile[:, None] >= tile_start[None, :]).astype(jnp.int32), axis=1) - 1
    tile_valid = (tile < jnp.sum(tiles_per)).astype(jnp.int32)
    zeros = jnp.zeros((1, MOE_ROWS), jnp.int32)
    src_tiles = src.reshape(n_tiles, MOE_ROWS)
    src_next = jnp.concatenate([src_tiles[1:], zeros, zeros], axis=0)
    first_dump = (n + jnp.arange(MOE_ROWS, dtype=jnp.int32)).reshape(1, MOE_ROWS)
    dest_prev = jnp.concatenate([first_dump, dest.reshape(n_tiles, MOE_ROWS)], axis=0)
    tile_expert = jnp.concatenate([tile_expert, tile_expert[-1:]])
    tile_valid = jnp.concatenate([tile_valid, jnp.zeros((1,), jnp.int32)])
    w_tiles = jnp.concatenate([w_slot.reshape(n_tiles, MOE_ROWS, 1), jnp.zeros((1, MOE_ROWS, 1), F32)], axis=0)
    return (tile_expert, tile_valid, src_tiles[0], src_next.reshape(n_tiles + 1, 1, MOE_ROWS),
            dest_prev.reshape(n_tiles + 1, 1, MOE_ROWS), w_tiles)


def _experts_kernel(te_ref, tv_ref, src0_ref, srcn_ref, dstp_ref, x_hbm, g_ref, ws_ref, w1_ref, w3_ref, w2_ref,
                    y_hbm, xbuf, hb_ref, acc_ref, obuf, sem_g, sem_s):
    j = pl.program_id(0)
    f = pl.program_id(1)
    n_f = pl.num_programs(1)
    valid = tv_ref[j] > 0
    out_slot = j % 2
    prev_slot = 1 - out_slot

    def gather_row(tok, r):
        return pltpu.make_async_copy(x_hbm.at[pl.ds(tok, 1), :], xbuf.at[pl.ds(r, 1), :], sem_g)

    def scatter_row(slot, r, d):
        return pltpu.make_async_copy(obuf.at[slot, pl.ds(r, 1), :], y_hbm.at[pl.ds(d, 1), :], sem_s.at[slot])

    def wait_gather():
        pltpu.make_async_copy(x_hbm.at[pl.ds(0, MOE_ROWS), :], xbuf, sem_g).wait()

    def wait_scatter(slot):
        pltpu.make_async_copy(obuf.at[slot], y_hbm.at[pl.ds(0, MOE_ROWS), :], sem_s.at[slot]).wait()

    @pl.when((j == 0) & (f == 0))
    def _():
        obuf[...] = jnp.zeros_like(obuf)

        def first(r, c):
            gather_row(src0_ref[r], r).start()
            return c
        lax.fori_loop(0, MOE_ROWS, first, 0)

    @pl.when(f == 0)
    def _():
        wait_gather()
        hb_ref[...] = _rms(xbuf[...], g_ref[...]).astype(BF16)
        acc_ref[...] = jnp.zeros_like(acc_ref)

    def row_dmas():
        base = f * MOE_DMA_CHUNK
        for c in range(MOE_DMA_CHUNK):
            r = base + c
            gather_row(srcn_ref[0, 0, r], r).start()
            scatter_row(prev_slot, r, dstp_ref[0, 0, r]).start()

    @pl.when(valid & (f < MOE_DMA_STEPS))
    def _():
        h = hb_ref[...]
        act = jax.nn.silu(_dot(h, w1_ref[...])) * _dot(h, w3_ref[...])
        acc_ref[...] += _dot(act.astype(BF16), w2_ref[...])
        row_dmas()

    @pl.when(valid & (f >= MOE_DMA_STEPS))
    def _():
        h = hb_ref[...]
        act = jax.nn.silu(_dot(h, w1_ref[...])) * _dot(h, w3_ref[...])
        acc_ref[...] += _dot(act.astype(BF16), w2_ref[...])

    @pl.when(jnp.logical_not(valid) & (f < MOE_DMA_STEPS))
    def _():
        row_dmas()

    @pl.when(f == n_f - 1)
    def _():
        @pl.when(j >= 1)
        def _():
            wait_scatter(out_slot)
        obuf[out_slot] = acc_ref[...] * ws_ref[...]

        @pl.when(j == pl.num_programs(0) - 1)
        def _():
            wait_scatter(prev_slot)
            wait_gather()


def _experts(x2, g, plan, w1, w3, w2, n_tiles):
    tile_expert, tile_valid, src0, src_next, dest_prev, w_tiles = plan
    t = x2.shape[0]
    d_ff = w1.shape[2]
    n_f = d_ff // MOE_FF_TILE
    assert n_f > MOE_DMA_STEPS and MOE_ROWS % MOE_DMA_STEPS == 0
    n_rows = (n_tiles + 1) * MOE_ROWS
    assert n_rows % TOP_K == 0 and n_rows >= TOP_K * t + MOE_ROWS

    def col(j, f, te, tv, s0):
        return f * tv[j] + (n_f - 1) * (1 - tv[j])

    grid_spec = pltpu.PrefetchScalarGridSpec(
        num_scalar_prefetch=3,
        grid=(n_tiles + 1, n_f),
        in_specs=[pl.BlockSpec((1, 1, MOE_ROWS), lambda j, f, te, tv, s0: (j, 0, 0), memory_space=pltpu.SMEM),
                  pl.BlockSpec((1, 1, MOE_ROWS), lambda j, f, te, tv, s0: (j, 0, 0), memory_space=pltpu.SMEM),
                  pl.BlockSpec(memory_space=pl.ANY),
                  pl.BlockSpec((1, D_MODEL), lambda j, f, te, tv, s0: (0, 0)),
                  pl.BlockSpec((None, MOE_ROWS, 1), lambda j, f, te, tv, s0: (j, 0, 0)),
                  pl.BlockSpec((None, D_MODEL, MOE_FF_TILE), lambda j, f, te, tv, s0: (te[j], 0, col(j, f, te, tv, s0))),
                  pl.BlockSpec((None, D_MODEL, MOE_FF_TILE), lambda j, f, te, tv, s0: (te[j], 0, col(j, f, te, tv, s0))),
                  pl.BlockSpec((None, MOE_FF_TILE, D_MODEL), lambda j, f, te, tv, s0: (te[j], col(j, f, te, tv, s0), 0))],
        out_specs=pl.BlockSpec(memory_space=pl.ANY),
        scratch_shapes=[pltpu.VMEM((MOE_ROWS, D_MODEL), F32),
                        pltpu.VMEM((MOE_ROWS, D_MODEL), BF16),
                        pltpu.VMEM((MOE_ROWS, D_MODEL), F32),
                        pltpu.VMEM((2, MOE_ROWS, D_MODEL), F32),
                        pltpu.SemaphoreType.DMA(()),
                        pltpu.SemaphoreType.DMA((2,))],
    )
    return pl.pallas_call(
        _experts_kernel,
        grid_spec=grid_spec,
        out_shape=jax.ShapeDtypeStruct((n_rows, D_MODEL), F32),
        compiler_params=pltpu.CompilerParams(dimension_semantics=("arbitrary", "arbitrary"),
                                             vmem_limit_bytes=VMEM_LIMIT),
        name="experts",
    )(tile_expert, tile_valid, src0, src_next, dest_prev, x2, g, w_tiles, w1, w3, w2)


def _combine_kernel(x_ref, y_ref, g_ref, o_ref):
    y = y_ref[...]
    o_ref[...] = _rms(x_ref[...] + y[:, :D_MODEL] + y[:, D_MODEL:], g_ref[...])


def _combine(x2, y_rows, g_final, tm=512):
    t = x2.shape[0]
    y_pairs = y_rows.reshape(y_rows.shape[0] // TOP_K, TOP_K * D_MODEL)
    return pl.pallas_call(
        _combine_kernel,
        grid=(t // tm,),
        in_specs=[pl.BlockSpec((tm, D_MODEL), lambda i: (i, 0)),
                  pl.BlockSpec((tm, TOP_K * D_MODEL), lambda i: (i, 0)),
                  pl.BlockSpec((1, D_MODEL), lambda i: (0, 0))],
        out_specs=pl.BlockSpec((tm, D_MODEL), lambda i: (i, 0)),
        out_shape=jax.ShapeDtypeStruct((t, D_MODEL), F32),
        compiler_params=_params(("parallel",)),
        name="combine",
    )(x2, y_pairs, g_final)


def _moe_ffn(x2, g, w_router, w1, w3, w2, g_final):
    t = x2.shape[0]
    n_tiles = (TOP_K * t + N_EXPERTS * (MOE_ROWS - 1)) // MOE_ROWS
    routed = _router(x2, g, w_router)
    plan = _moe_plan(routed, n_tiles)
    y_rows = _experts(x2, g, plan, w1, w3, w2, n_tiles)
    return _combine(x2, y_rows, g_final)


def kernel(x, norm_mix, w_in, gmlp_ln_g, gmlp_ln_b, gmlp_w_s, gmlp_b_s, pool_w, pool_b, pool_scale, conv_w, conv_b, conv_ln_g, conv_ln_b, w_br_a, w_br_b, w_br_c, w_br_d, w_gate, b_gate, w_out, norm_ffn, dense_w1, dense_w3, dense_w2, moe_router, moe_w1, moe_w3, moe_w2, norm_final):
    b, s, d = x.shape
    depth = norm_mix.shape[0]
    assert depth == 2 and d == D_MODEL, "kernel is written for the two-layer (dense, MoE) trunk"
    x2 = x.reshape(b * s, d)
    attn_bias = _attention_bias(s)
    row = lambda a: a.reshape(1, -1)

    for l in range(depth):
        za, zb, qkv, zd = _inproj(x2, row(norm_mix[l]), w_in[l].astype(BF16))

        w_cat = gmlp_w_s[l].transpose(1, 0, 2).reshape(GMLP_CHUNK, GMLP_GROUPS * GMLP_CHUNK).astype(BF16)
        bias_a = jnp.repeat(gmlp_b_s[l].T, GMLP_WIDTH // GMLP_GROUPS, axis=1)
        ya = _gmlp(za, row(gmlp_ln_g[l]), row(gmlp_ln_b[l]), w_cat, bias_a)

        w_bd = jax.scipy.linalg.block_diag(*[pool_w[l, g] for g in range(len(POOL_WINDOWS))]).astype(BF16)
        yb = _pool(zb, w_bd, row(pool_b[l]), row(pool_scale[l]), s)

        yc = _attention(qkv, attn_bias, s)
        yd = _conv(zd, conv_w[l], row(conv_b[l]), row(conv_ln_g[l]), row(conv_ln_b[l]), s)

        x2 = _merge(x2, row(norm_mix[l]), ya, yb, yc, yd, w_gate[l].astype(BF16), row(b_gate[l]),
                    w_br_a[l].astype(BF16), w_br_b[l].astype(BF16), w_br_c[l].astype(BF16),
                    w_br_d[l].astype(BF16), w_out[l].astype(BF16))

        i = l // 2
        if l % 2 == 0:
            x2 = _dense_ffn(x2, row(norm_ffn[l]), dense_w1[i].astype(BF16), dense_w3[i].astype(BF16),
                            dense_w2[i].astype(BF16))
        else:
            w_router = jnp.pad(moe_router[i], ((0, 0), (0, LANES - N_EXPERTS)))
            x2 = _moe_ffn(x2, row(norm_ffn[l]), w_router, moe_w1[i].astype(BF16), moe_w3[i].astype(BF16),
                          moe_w2[i].astype(BF16), row(norm_final))
    return x2.reshape(b, s, d)
```

```python
import functools
import math

import jax
import jax.numpy as jnp
import numpy as np
from jax import lax
from jax.experimental import pallas as pl
from jax.experimental.pallas import tpu as pltpu

F32 = jnp.float32
BF16 = jnp.bfloat16

D_MODEL = 1024
GMLP_WIDTH = 256
GMLP_GROUPS = 4
GMLP_CHUNK = 128
POOL_WIDTH = 256
POOL_WINDOWS = (2, 4, 8, 16)
POOL_GROUP = 64
HEAD_DIM = 64
ATTN_HEADS = 8
ATTN_WIDTH = 512
DILATED_CONFIGS = ((128, 1), (512, 4), (2048, 16))
ATTN_BLOCK = 128
ALIBI_MAX_BIAS = 8.0
MASK_VALUE = -1e30
CONV_WIDTH = 256
CONV_KERNEL = 31
N_EXPERTS = 8
RMS_EPS = 1e-6
LN_EPS = 1e-5

A_END = 2 * GMLP_WIDTH
B_END = A_END + POOL_WIDTH
Q_END = B_END + ATTN_WIDTH
V_END = Q_END + 2 * ATTN_WIDTH
IN_WIDTH = V_END + 2 * CONV_WIDTH

LANES = 128
VMEM_LIMIT = 56 * 1024 * 1024


def _params(semantics):
    return pltpu.CompilerParams(dimension_semantics=semantics, vmem_limit_bytes=VMEM_LIMIT)


def _rms(x, g):
    return x * lax.rsqrt(jnp.mean(x * x, axis=-1, keepdims=True) + RMS_EPS) * g


def _layernorm(x, g, b):
    mu = jnp.mean(x, axis=-1, keepdims=True)
    xc = x - mu
    var = jnp.mean(xc * xc, axis=-1, keepdims=True)
    return xc * lax.rsqrt(var + LN_EPS) * g + b


def _dot(a, b):
    return jnp.dot(a, b, preferred_element_type=F32)


def _inproj_kernel(x_ref, g_ref, w_ref, za_ref, zb_ref, qkv_ref, zd_ref):
    h = _rms(x_ref[...], g_ref[...]).astype(BF16)
    za_ref[...] = _dot(h, w_ref[:, :A_END])
    zb_ref[...] = _dot(h, w_ref[:, A_END:B_END])
    q = _dot(h, w_ref[:, B_END:Q_END]) * (1.0 / math.sqrt(HEAD_DIM))
    qkv_ref[:, :ATTN_WIDTH] = q.astype(BF16)
    qkv_ref[:, ATTN_WIDTH:] = _dot(h, w_ref[:, Q_END:V_END]).astype(BF16)
    zd_ref[...] = _dot(h, w_ref[:, V_END:])


def _inproj(x2, g, w, tm=512):
    t = x2.shape[0]
    row = lambda i: (i, 0)
    fixed = lambda i: (0, 0)
    return pl.pallas_call(
        _inproj_kernel,
        grid=(t // tm,),
        in_specs=[pl.BlockSpec((tm, D_MODEL), row),
                  pl.BlockSpec((1, D_MODEL), fixed),
                  pl.BlockSpec((D_MODEL, IN_WIDTH), fixed)],
        out_specs=[pl.BlockSpec((tm, A_END), row),
                   pl.BlockSpec((tm, POOL_WIDTH), row),
                   pl.BlockSpec((tm, 3 * ATTN_WIDTH), row),
                   pl.BlockSpec((tm, 2 * CONV_WIDTH), row)],
        out_shape=[jax.ShapeDtypeStruct((t, A_END), F32),
                   jax.ShapeDtypeStruct((t, POOL_WIDTH), F32),
                   jax.ShapeDtypeStruct((t, 3 * ATTN_WIDTH), BF16),
                   jax.ShapeDtypeStruct((t, 2 * CONV_WIDTH), F32)],
        compiler_params=_params(("parallel",)),
        name="inproj",
    )(x2, g, w)


def _gmlp_kernel(z_ref, g_ref, b_ref, w_ref, bias_ref, o_ref, *, chunks):
    gw = GMLP_WIDTH // GMLP_GROUPS
    row = lax.broadcasted_iota(jnp.int32, (GMLP_CHUNK, GMLP_GROUPS * GMLP_CHUNK), 0)
    col = lax.broadcasted_iota(jnp.int32, (GMLP_CHUNK, GMLP_GROUPS * GMLP_CHUNK), 1)
    w = w_ref[...]
    w = jnp.where((col & (GMLP_CHUNK - 1)) <= row, w, jnp.zeros_like(w))
    lane_group = lax.broadcasted_iota(jnp.int32, (GMLP_CHUNK, GMLP_WIDTH), 1) // gw
    for c in range(chunks):
        z = jax.nn.gelu(z_ref[c * GMLP_CHUNK:(c + 1) * GMLP_CHUNK, :])
        u = z[:, :GMLP_WIDTH]
        v = _layernorm(z[:, GMLP_WIDTH:], g_ref[...], b_ref[...]).astype(BF16)
        stacked = jnp.concatenate(
            [jnp.where(lane_group == g, v, jnp.zeros_like(v)) for g in range(GMLP_GROUPS)], axis=0)
        mixed = _dot(w, stacked) + bias_ref[...]
        o_ref[c * GMLP_CHUNK:(c + 1) * GMLP_CHUNK, :] = (u * mixed).astype(BF16)


def _gmlp(za, ln_g, ln_b, w_cat, bias, tm=512):
    t = za.shape[0]
    row = lambda i: (i, 0)
    fixed = lambda i: (0, 0)
    return pl.pallas_call(
        functools.partial(_gmlp_kernel, chunks=tm // GMLP_CHUNK),
        grid=(t // tm,),
        in_specs=[pl.BlockSpec((tm, A_END), row),
                  pl.BlockSpec((1, GMLP_WIDTH), fixed),
                  pl.BlockSpec((1, GMLP_WIDTH), fixed),
                  pl.BlockSpec((GMLP_CHUNK, GMLP_GROUPS * GMLP_CHUNK), fixed),
                  pl.BlockSpec((GMLP_CHUNK, GMLP_WIDTH), fixed)],
        out_specs=pl.BlockSpec((tm, GMLP_WIDTH), row),
        out_shape=jax.ShapeDtypeStruct((t, GMLP_WIDTH), BF16),
        compiler_params=_params(("parallel",)),
        name="gmlp",
    )(za, ln_g, ln_b, w_cat, bias)


def _pool_kernel(z_ref, w_ref, b_ref, s_ref, o_ref):
    x = z_ref[...]
    seq = x.shape[0]
    t = lax.broadcasted_iota(jnp.int32, x.shape, 0)
    lane = lax.broadcasted_iota(jnp.int32, x.shape, 1)

    def lagged(a, k):
        return jnp.where(t >= k, pltpu.roll(a, k, 0), 0.0)

    sums = []
    s = x
    for w in POOL_WINDOWS:
        s = s + lagged(s, w // 2)
        sums.append(s)
    group = lane // POOL_GROUP
    sel = sums[-1]
    width = jnp.full(x.shape, float(POOL_WINDOWS[-1]), F32)
    for g in range(len(POOL_WINDOWS) - 2, -1, -1):
        sel = jnp.where(group == g, sums[g], sel)
        width = jnp.where(group == g, float(POOL_WINDOWS[g]), width)
    count = jnp.minimum((t + 1).astype(F32), width)
    pooled = sel / count - x
    y = _dot(pooled.astype(BF16), w_ref[...]) + b_ref[...]
    o_ref[...] = (y * s_ref[...]).astype(BF16)
    del seq


def _pool(zb, w_bd, b, scale, seq):
    t = zb.shape[0]
    row = lambda i: (i, 0)
    fixed = lambda i: (0, 0)
    return pl.pallas_call(
        _pool_kernel,
        grid=(t // seq,),
        in_specs=[pl.BlockSpec((seq, POOL_WIDTH), row),
                  pl.BlockSpec((POOL_WIDTH, POOL_WIDTH), fixed),
                  pl.BlockSpec((1, POOL_WIDTH), fixed),
                  pl.BlockSpec((1, POOL_WIDTH), fixed)],
        out_specs=pl.BlockSpec((seq, POOL_WIDTH), row),
        out_shape=jax.ShapeDtypeStruct((t, POOL_WIDTH), BF16),
        compiler_params=_params(("parallel",)),
        name="pool",
    )(zb, w_bd, b, scale)


CONV_PAD = 32
CONV_ROWS = 256


def _conv_kernel(z_ref, w_ref, b_ref, g_ref, be_ref, o_ref, pad_ref):
    seq = o_ref.shape[0]
    pad_ref[:CONV_PAD, :] = jnp.zeros((CONV_PAD, CONV_WIDTH), F32)
    pad_ref[CONV_PAD:, :] = z_ref[:, :CONV_WIDTH] * jax.nn.sigmoid(z_ref[:, CONV_WIDTH:])
    first = CONV_PAD - (CONV_KERNEL - 1)
    for r in range(seq // CONV_ROWS):
        r0 = r * CONV_ROWS
        acc = jnp.zeros((CONV_ROWS, CONV_WIDTH), F32)
        for j in range(CONV_KERNEL):
            acc = acc + pad_ref[r0 + first + j:r0 + first + j + CONV_ROWS, :] * w_ref[j:j + 1, :]
        y = _layernorm(acc + b_ref[...], g_ref[...], be_ref[...])
        o_ref[r0:r0 + CONV_ROWS, :] = jax.nn.silu(y).astype(BF16)


def _conv(zd, w, b, ln_g, ln_b, seq):
    t = zd.shape[0]
    row = lambda i: (i, 0)
    fixed = lambda i: (0, 0)
    return pl.pallas_call(
        _conv_kernel,
        grid=(t // seq,),
        in_specs=[pl.BlockSpec((seq, 2 * CONV_WIDTH), row),
                  pl.BlockSpec((CONV_KERNEL, CONV_WIDTH), fixed),
                  pl.BlockSpec((1, CONV_WIDTH), fixed),
                  pl.BlockSpec((1, CONV_WIDTH), fixed),
                  pl.BlockSpec((1, CONV_WIDTH), fixed)],
        out_specs=pl.BlockSpec((seq, CONV_WIDTH), row),
        out_shape=jax.ShapeDtypeStruct((t, CONV_WIDTH), BF16),
        scratch_shapes=[pltpu.VMEM((seq + CONV_PAD, CONV_WIDTH), F32)],
        compiler_params=_params(("parallel",)),
        name="conv",
    )(zd, w, b, ln_g, ln_b)


ATTN_Q_ROWS = 256
HEADS_PER_PAIR = LANES // HEAD_DIM


def _attention_bias(seq):
    qi = jnp.arange(ATTN_Q_ROWS, dtype=jnp.int32)[:, None]
    x = jnp.arange(seq, dtype=jnp.int32)[None, :]
    dist = seq - ATTN_Q_ROWS + qi - x
    mult = jnp.zeros(dist.shape, F32)
    for window, dilation in DILATED_CONFIGS:
        mult = mult + ((dist >= 0) & (dist % dilation == 0) & (dist <= window)).astype(F32)
    slopes = 2.0 ** (-ALIBI_MAX_BIAS * jnp.arange(1, ATTN_HEADS + 1, dtype=F32) / ATTN_HEADS)
    alibi = -slopes[:, None, None] * dist.astype(F32)[None]
    bias = jnp.where(mult > 0, jnp.log(jnp.maximum(mult, 1.0)) + alibi, MASK_VALUE)
    return bias.reshape(ATTN_HEADS * ATTN_Q_ROWS, seq)


def _attn_kernel(q_ref, k_ref, v_ref, bias_ref, o_ref):
    seq = q_ref.shape[0]
    lane = lax.broadcasted_iota(jnp.int32, (ATTN_Q_ROWS, LANES), 1)
    first_head = lane < HEAD_DIM
    nt = (((1,), (1,)), ((), ()))
    for i in range(seq // ATTN_Q_ROWS):
        rows = slice(i * ATTN_Q_ROWS, (i + 1) * ATTN_Q_ROWS)
        n_keys = (i + 1) * ATTN_Q_ROWS
        q = q_ref[rows, :]
        zero = jnp.zeros_like(q)
        stacked = jnp.concatenate([jnp.where(first_head, q, zero), jnp.where(first_head, zero, q)], axis=0)
        s = lax.dot_general(stacked, k_ref[:n_keys, :], nt, preferred_element_type=F32)
        s = s + bias_ref[:, seq - n_keys:]
        p = jnp.exp(s - jnp.max(s, axis=-1, keepdims=True))
        den = jnp.sum(p, axis=-1, keepdims=True)
        o = _dot(p.astype(BF16), v_ref[:n_keys, :]) / den
        o_ref[rows, :] = jnp.where(first_head, o[:ATTN_Q_ROWS], o[ATTN_Q_ROWS:]).astype(BF16)


def _attention(qkv, bias, seq):
    t = qkv.shape[0]
    pairs = ATTN_WIDTH // LANES
    return pl.pallas_call(
        _attn_kernel,
        grid=(pairs, t // seq),
        in_specs=[pl.BlockSpec((seq, LANES), lambda p, b: (b, p)),
                  pl.BlockSpec((seq, LANES), lambda p, b: (b, pairs + p)),
                  pl.BlockSpec((seq, LANES), lambda p, b: (b, 2 * pairs + p)),
                  pl.BlockSpec((HEADS_PER_PAIR * ATTN_Q_ROWS, seq), lambda p, b: (p, 0))],
        out_specs=pl.BlockSpec((seq, LANES), lambda p, b: (b, p)),
        out_shape=jax.ShapeDtypeStruct((t, ATTN_WIDTH), BF16),
        compiler_params=_params(("parallel", "parallel")),
        name="attention",
    )(qkv, qkv, qkv, bias)


def _merge_kernel(x_ref, g_ref, ya_ref, yb_ref, yc_ref, yd_ref, wg_ref, bg_ref,
                  wa_ref, wb_ref, wc_ref, wd_ref, wo_ref, o_ref):
    x = x_ref[...]
    h = _rms(x, g_ref[...]).astype(BF16)
    merged = None
    for i, (y_ref, w_ref) in enumerate(((ya_ref, wa_ref), (yb_ref, wb_ref), (yc_ref, wc_ref), (yd_ref, wd_ref))):
        cols = slice(i * D_MODEL, (i + 1) * D_MODEL)
        gate = jax.nn.sigmoid(_dot(h, wg_ref[:, cols]) + bg_ref[:, cols])
        term = gate * _dot(y_ref[...], w_ref[...])
        merged = term if merged is None else merged + term
    o_ref[...] = x + _dot(merged.astype(BF16), wo_ref[...])


def _merge(x2, g, ya, yb, yc, yd, wg, bg, wa, wb, wc, wd, wo, tm=512):
    t = x2.shape[0]
    row = lambda i: (i, 0)
    fixed = lambda i: (0, 0)
    full = lambda a: pl.BlockSpec(a.shape, fixed)
    return pl.pallas_call(
        _merge_kernel,
        grid=(t // tm,),
        in_specs=[pl.BlockSpec((tm, D_MODEL), row), full(g),
                  pl.BlockSpec((tm, GMLP_WIDTH), row), pl.BlockSpec((tm, POOL_WIDTH), row),
                  pl.BlockSpec((tm, ATTN_WIDTH), row), pl.BlockSpec((tm, CONV_WIDTH), row),
                  full(wg), full(bg), full(wa), full(wb), full(wc), full(wd), full(wo)],
        out_specs=pl.BlockSpec((tm, D_MODEL), row),
        out_shape=jax.ShapeDtypeStruct((t, D_MODEL), F32),
        compiler_params=_params(("parallel",)),
        name="merge",
    )(x2, g, ya, yb, yc, yd, wg, bg, wa, wb, wc, wd, wo)


def _dense_ffn_kernel(x_ref, g_ref, w1_ref, w3_ref, w2_ref, o_ref, h_ref, acc_ref):
    f = pl.program_id(1)

    @pl.when(f == 0)
    def _():
        h_ref[...] = _rms(x_ref[...], g_ref[...]).astype(BF16)
        acc_ref[...] = jnp.zeros_like(acc_ref)

    h = h_ref[...]
    act = jax.nn.silu(_dot(h, w1_ref[...])) * _dot(h, w3_ref[...])
    acc_ref[...] += _dot(act.astype(BF16), w2_ref[...])

    @pl.when(f == pl.num_programs(1) - 1)
    def _():
        o_ref[...] = x_ref[...] + acc_ref[...]


def _dense_ffn(x2, g, w1, w3, w2, tm=1024, tf=256):
    t = x2.shape[0]
    d_ff = w1.shape[1]
    return pl.pallas_call(
        _dense_ffn_kernel,
        grid=(t // tm, d_ff // tf),
        in_specs=[pl.BlockSpec((tm, D_MODEL), lambda i, f: (i, 0)),
                  pl.BlockSpec((1, D_MODEL), lambda i, f: (0, 0)),
                  pl.BlockSpec((D_MODEL, tf), lambda i, f: (0, f)),
                  pl.BlockSpec((D_MODEL, tf), lambda i, f: (0, f)),
                  pl.BlockSpec((tf, D_MODEL), lambda i, f: (f, 0))],
        out_specs=pl.BlockSpec((tm, D_MODEL), lambda i, f: (i, 0)),
        out_shape=jax.ShapeDtypeStruct((t, D_MODEL), F32),
        scratch_shapes=[pltpu.VMEM((tm, D_MODEL), BF16), pltpu.VMEM((tm, D_MODEL), F32)],
        compiler_params=_params(("parallel", "arbitrary")),
        name="dense_ffn",
    )(x2, g, w1, w3, w2)


TOP_K = 2


def _router_kernel(x_ref, g_ref, wr_ref, o_ref):
    h = _rms(x_ref[...], g_ref[...])
    logits = jnp.dot(h, wr_ref[...], preferred_element_type=F32, precision=lax.Precision.HIGHEST)
    lane = lax.broadcasted_iota(jnp.int32, logits.shape, 1)
    neg = jnp.float32(-jnp.inf)
    logits = jnp.where(lane < N_EXPERTS, logits, neg)
    m1 = jnp.max(logits, axis=-1, keepdims=True)
    i1 = jnp.min(jnp.where(logits == m1, lane, LANES), axis=-1, keepdims=True)
    rest = jnp.where(lane == i1, neg, logits)
    m2 = jnp.max(rest, axis=-1, keepdims=True)
    i2 = jnp.min(jnp.where(rest == m2, lane, LANES), axis=-1, keepdims=True)
    e2 = jnp.exp(m2 - m1)
    den = 1.0 + e2
    o_ref[...] = jnp.where(lane == 0, i1.astype(F32),
                           jnp.where(lane == 1, i2.astype(F32),
                                     jnp.where(lane == 2, 1.0 / den, jnp.where(lane == 3, e2 / den, 0.0))))


def _router(x2, g, w_router, tm=1024):
    t = x2.shape[0]
    return pl.pallas_call(
        _router_kernel,
        grid=(t // tm,),
        in_specs=[pl.BlockSpec((tm, D_MODEL), lambda i: (i, 0)),
                  pl.BlockSpec((1, D_MODEL), lambda i: (0, 0)),
                  pl.BlockSpec((D_MODEL, LANES), lambda i: (0, 0))],
        out_specs=pl.BlockSpec((tm, LANES), lambda i: (i, 0)),
        out_shape=jax.ShapeDtypeStruct((t, LANES), F32),
        compiler_params=_params(("parallel",)),
        name="router",
    )(x2, g, w_router)


MOE_ROWS = 768
MOE_FF_TILE = 512
MOE_DMA_STEPS = 6
MOE_DMA_CHUNK = MOE_ROWS // MOE_DMA_STEPS


def _moe_plan(routed, n_tiles):
    t = routed.shape[0]
    n = TOP_K * t
    expert = routed[:, :TOP_K].astype(jnp.int32).reshape(n)
    assignment = jnp.arange(n, dtype=jnp.int32)
    order = jnp.sort(expert * n + assignment) % n
    experts = jnp.arange(N_EXPERTS, dtype=jnp.int32)
    counts = jnp.sum((expert[:, None] == experts[None, :]).astype(jnp.int32), axis=0)
    first_sorted = jnp.cumsum(counts) - counts
    tiles_per = (counts + MOE_ROWS - 1) // MOE_ROWS
    tile_start = jnp.cumsum(tiles_per) - tiles_per
    tile = jnp.arange(n_tiles, dtype=jnp.int32)
    tile_expert = jnp.sum((tile[:, None] >= tile_start[None, :]).astype(jnp.int32), axis=1) - 1
    tile_valid = (tile < jnp.sum(tiles_per)).astype(jnp.int32)
    n_slots = n_tiles * MOE_ROWS
    slot = jnp.arange(n_slots, dtype=jnp.int32)
    slot_expert = jnp.repeat(tile_expert, MOE_ROWS)
    owner = slot_expert[:, None] == experts[None, :]
    pick = lambda v: jnp.sum(jnp.where(owner, v[None, :], 0), axis=1)
    rank = slot - pick(tile_start) * MOE_ROWS
    used = (rank < pick(counts)) & (jnp.repeat(tile_valid, MOE_ROWS) > 0)
    assign = order[jnp.clip(pick(first_sorted) + rank, 0, n - 1)]
    token = assign // TOP_K
    src = jnp.where(used, token, 0)
    dump = n + MOE_ROWS - 1 + jnp.cumsum(jnp.logical_not(used).astype(jnp.int32))
    dest = jnp.where(used, (assign % TOP_K) * t + token, dump)
    zeros = jnp.zeros((1, MOE_ROWS), jnp.int32)
    src_tiles = src.reshape(n_tiles, MOE_ROWS)
    src_next = jnp.concatenate([src_tiles[1:], zeros, zeros], axis=0)
    first_dump = (n + jnp.arange(MOE_ROWS, dtype=jnp.int32)).reshape(1, MOE_ROWS)
    dest_prev = jnp.concatenate([first_dump, dest.reshape(n_tiles, MOE_ROWS)], axis=0)
    tile_expert = jnp.concatenate([tile_expert, tile_expert[-1:]])
    tile_valid = jnp.concatenate([tile_valid, jnp.zeros((1,), jnp.int32)])
    return (tile_expert, tile_valid, src_tiles[0], src_next.reshape(n_tiles + 1, 1, MOE_ROWS),
            dest_prev.reshape(n_tiles + 1, 1, MOE_ROWS))


def _experts_kernel(te_ref, tv_ref, src0_ref, srcn_ref, dstp_ref, x_hbm, g_ref, w1_ref, w3_ref, w2_ref,
                    y_hbm, xbuf, hb_ref, acc_ref, obuf, sem_g, sem_s):
    j = pl.program_id(0)
    f = pl.program_id(1)
    n_f = pl.num_programs(1)
    valid = tv_ref[j] > 0
    out_slot = j % 2
    prev_slot = 1 - out_slot

    def gather_row(tok, r):
        return pltpu.make_async_copy(x_hbm.at[pl.ds(tok, 1), :], xbuf.at[pl.ds(r, 1), :], sem_g)

    def scatter_row(slot, r, d):
        return pltpu.make_async_copy(obuf.at[slot, pl.ds(r, 1), :], y_hbm.at[pl.ds(d, 1), :], sem_s.at[slot])

    def wait_gather():
        pltpu.make_async_copy(x_hbm.at[pl.ds(0, MOE_ROWS), :], xbuf, sem_g).wait()

    def wait_scatter(slot):
        pltpu.make_async_copy(obuf.at[slot], y_hbm.at[pl.ds(0, MOE_ROWS), :], sem_s.at[slot]).wait()

    @pl.when((j == 0) & (f == 0))
    def _():
        obuf[...] = jnp.zeros_like(obuf)

        def first(r, c):
            gather_row(src0_ref[r], r).start()
            return c
        lax.fori_loop(0, MOE_ROWS, first, 0)

    @pl.when(f == 0)
    def _():
        wait_gather()
        hb_ref[...] = _rms(xbuf[...], g_ref[...]).astype(BF16)
        acc_ref[...] = jnp.zeros_like(acc_ref)

    def row_dmas():
        base = pl.multiple_of(f * MOE_DMA_CHUNK, MOE_DMA_CHUNK)
        for c in range(MOE_DMA_CHUNK):
            r = base + c
            gather_row(srcn_ref[0, 0, r], r).start()
            scatter_row(prev_slot, r, dstp_ref[0, 0, r]).start()

    @pl.when(valid & (f < MOE_DMA_STEPS))
    def _():
        row_dmas()
        h = hb_ref[...]
        act = jax.nn.silu(_dot(h, w1_ref[...])) * _dot(h, w3_ref[...])
        acc_ref[...] += _dot(act.astype(BF16), w2_ref[...])

    @pl.when(valid & (f >= MOE_DMA_STEPS))
    def _():
        h = hb_ref[...]
        act = jax.nn.silu(_dot(h, w1_ref[...])) * _dot(h, w3_ref[...])
        acc_ref[...] += _dot(act.astype(BF16), w2_ref[...])

    @pl.when(jnp.logical_not(valid) & (f < MOE_DMA_STEPS))
    def _():
        row_dmas()

    @pl.when(f == n_f - 1)
    def _():
        @pl.when(j >= 1)
        def _():
            wait_scatter(out_slot)
        obuf[out_slot] = acc_ref[...]

        @pl.when(j == pl.num_programs(0) - 1)
        def _():
            wait_scatter(prev_slot)
            wait_gather()


def _experts(x2, g, plan, w1, w3, w2, n_tiles):
    tile_expert, tile_valid, src0, src_next, dest_prev = plan
    t = x2.shape[0]
    d_ff = w1.shape[2]
    n_f = d_ff // MOE_FF_TILE
    assert n_f > MOE_DMA_STEPS and MOE_ROWS % MOE_DMA_STEPS == 0
    n_rows = (n_tiles + 1) * MOE_ROWS
    assert n_rows % TOP_K == 0 and n_rows >= TOP_K * t + MOE_ROWS

    def col(j, f, te, tv, s0):
        return f * tv[j] + (n_f - 1) * (1 - tv[j])

    grid_spec = pltpu.PrefetchScalarGridSpec(
        num_scalar_prefetch=3,
        grid=(n_tiles + 1, n_f),
        in_specs=[pl.BlockSpec((1, 1, MOE_ROWS), lambda j, f, te, tv, s0: (j, 0, 0), memory_space=pltpu.SMEM),
                  pl.BlockSpec((1, 1, MOE_ROWS), lambda j, f, te, tv, s0: (j, 0, 0), memory_space=pltpu.SMEM),
                  pl.BlockSpec(memory_space=pl.ANY),
                  pl.BlockSpec((1, D_MODEL), lambda j, f, te, tv, s0: (0, 0)),
                  pl.BlockSpec((None, D_MODEL, MOE_FF_TILE), lambda j, f, te, tv, s0: (te[j], 0, col(j, f, te, tv, s0))),
                  pl.BlockSpec((None, D_MODEL, MOE_FF_TILE), lambda j, f, te, tv, s0: (te[j], 0, col(j, f, te, tv, s0))),
                  pl.BlockSpec((None, MOE_FF_TILE, D_MODEL), lambda j, f, te, tv, s0: (te[j], col(j, f, te, tv, s0), 0))],
        out_specs=pl.BlockSpec(memory_space=pl.ANY),
        scratch_shapes=[pltpu.VMEM((MOE_ROWS, D_MODEL), F32),
                        pltpu.VMEM((MOE_ROWS, D_MODEL), BF16),
                        pltpu.VMEM((MOE_ROWS, D_MODEL), F32),
                        pltpu.VMEM((2, MOE_ROWS, D_MODEL), F32),
                        pltpu.SemaphoreType.DMA(()),
                        pltpu.SemaphoreType.DMA((2,))],
    )
    return pl.pallas_call(
        _experts_kernel,
        grid_spec=grid_spec,
        out_shape=jax.ShapeDtypeStruct((n_rows, D_MODEL), F32),
        compiler_params=pltpu.CompilerParams(dimension_semantics=("arbitrary", "arbitrary"),
                                             vmem_limit_bytes=VMEM_LIMIT),
        name="experts",
    )(tile_expert, tile_valid, src0, src_next, dest_prev, x2, g, w1, w3, w2)


def _combine_kernel(x_ref, r_ref, y0_ref, y1_ref, g_ref, o_ref):
    routed = r_ref[...]
    w0 = routed[:, TOP_K:TOP_K + 1]
    w1 = routed[:, TOP_K + 1:TOP_K + 2]
    o_ref[...] = _rms(x_ref[...] + w0 * y0_ref[...] + w1 * y1_ref[...], g_ref[...])


def _combine(x2, routed, y_rows, g_final, tm=512):
    t = x2.shape[0]
    second = t // tm
    return pl.pallas_call(
        _combine_kernel,
        grid=(t // tm,),
        in_specs=[pl.BlockSpec((tm, D_MODEL), lambda i: (i, 0)),
                  pl.BlockSpec((tm, LANES), lambda i: (i, 0)),
                  pl.BlockSpec((tm, D_MODEL), lambda i: (i, 0)),
                  pl.BlockSpec((tm, D_MODEL), lambda i: (second + i, 0)),
                  pl.BlockSpec((1, D_MODEL), lambda i: (0, 0))],
        out_specs=pl.BlockSpec((tm, D_MODEL), lambda i: (i, 0)),
        out_shape=jax.ShapeDtypeStruct((t, D_MODEL), F32),
        compiler_params=_params(("parallel",)),
        name="combine",
    )(x2, routed, y_rows, y_rows, g_final)


def _moe_ffn(x2, g, w_router, w1, w3, w2, g_final):
    t = x2.shape[0]
    n_tiles = (TOP_K * t + N_EXPERTS * (MOE_ROWS - 1)) // MOE_ROWS
    routed = _router(x2, g, w_router)
    plan = _moe_plan(routed, n_tiles)
    y_rows = _experts(x2, g, plan, w1, w3, w2, n_tiles)
    return _combine(x2, routed, y_rows, g_final)


def kernel(x, norm_mix, w_in, gmlp_ln_g, gmlp_ln_b, gmlp_w_s, gmlp_b_s, pool_w, pool_b, pool_scale, conv_w, conv_b, conv_ln_g, conv_ln_b, w_br_a, w_br_b, w_br_c, w_br_d, w_gate, b_gate, w_out, norm_ffn, dense_w1, dense_w3, dense_w2, moe_router, moe_w1, moe_w3, moe_w2, norm_final):
    b, s, d = x.shape
    depth = norm_mix.shape[0]
    assert depth == 2 and d == D_MODEL, "kernel is written for the two-layer (dense, MoE) trunk"
    x2 = x.reshape(b * s, d)
    attn_bias = _attention_bias(s)
    row = lambda a: a.reshape(1, -1)

    for l in range(depth):
        za, zb, qkv, zd = _inproj(x2, row(norm_mix[l]), w_in[l].astype(BF16))

        w_cat = gmlp_w_s[l].transpose(1, 0, 2).reshape(GMLP_CHUNK, GMLP_GROUPS * GMLP_CHUNK).astype(BF16)
        bias_a = jnp.repeat(gmlp_b_s[l].T, GMLP_WIDTH // GMLP_GROUPS, axis=1)
        ya = _gmlp(za, row(gmlp_ln_g[l]), row(gmlp_ln_b[l]), w_cat, bias_a)

        w_bd = jax.scipy.linalg.block_diag(*[pool_w[l, g] for g in range(len(POOL_WINDOWS))]).astype(BF16)
        yb = _pool(zb, w_bd, row(pool_b[l]), row(pool_scale[l]), s)

        yc = _attention(qkv, attn_bias, s)
        yd = _conv(zd, conv_w[l], row(conv_b[l]), row(conv_ln_g[l]), row(conv_ln_b[l]), s)

        x2 = _merge(x2, row(norm_mix[l]), ya, yb, yc, yd, w_gate[l].astype(BF16), row(b_gate[l]),
                    w_br_a[l].astype(BF16), w_br_b[l].astype(BF16), w_br_c[l].astype(BF16),
                    w_br_d[l].astype(BF16), w_out[l].astype(BF16))

        i = l // 2
        if l % 2 == 0:
            x2 = _dense_ffn(x2, row(norm_ffn[l]), dense_w1[i].astype(BF16), dense_w3[i].astype(BF16),
                            dense_w2[i].astype(BF16))
        else:
            w_router = jnp.pad(moe_router[i], ((0, 0), (0, LANES - N_EXPERTS)))
            x2 = _moe_ffn(x2, row(norm_ffn[l]), w_router, moe_w1[i].astype(BF16), moe_w3[i].astype(BF16),
                          moe_w2[i].astype(BF16), row(norm_final))
    return x2.reshape(b, s, d)
```

```python
import functools
import math

import jax
import jax.numpy as jnp
import numpy as np
from jax import lax
from jax.experimental import pallas as pl
from jax.experimental.pallas import tpu as pltpu

F32 = jnp.float32
BF16 = jnp.bfloat16

D_MODEL = 1024
GMLP_WIDTH = 256
GMLP_GROUPS = 4
GMLP_CHUNK = 128
POOL_WIDTH = 256
POOL_WINDOWS = (2, 4, 8, 16)
POOL_GROUP = 64
HEAD_DIM = 64
ATTN_HEADS = 8
ATTN_WIDTH = 512
DILATED_CONFIGS = ((128, 1), (512, 4), (2048, 16))
ATTN_BLOCK = 128
ALIBI_MAX_BIAS = 8.0
MASK_VALUE = -1e30
CONV_WIDTH = 256
CONV_KERNEL = 31
N_EXPERTS = 8
RMS_EPS = 1e-6
LN_EPS = 1e-5

A_END = 2 * GMLP_WIDTH
B_END = A_END + POOL_WIDTH
Q_END = B_END + ATTN_WIDTH
V_END = Q_END + 2 * ATTN_WIDTH
IN_WIDTH = V_END + 2 * CONV_WIDTH

LANES = 128
VMEM_LIMIT = 56 * 1024 * 1024


def _params(semantics):
    return pltpu.CompilerParams(dimension_semantics=semantics, vmem_limit_bytes=VMEM_LIMIT)


def _rms(x, g):
    return x * lax.rsqrt(jnp.mean(x * x, axis=-1, keepdims=True) + RMS_EPS) * g


def _layernorm(x, g, b):
    mu = jnp.mean(x, axis=-1, keepdims=True)
    xc = x - mu
    var = jnp.mean(xc * xc, axis=-1, keepdims=True)
    return xc * lax.rsqrt(var + LN_EPS) * g + b


def _dot(a, b):
    return jnp.dot(a, b, preferred_element_type=F32)


def _inproj_kernel(x_ref, g_ref, w_ref, za_ref, zb_ref, qkv_ref, zd_ref):
    h = _rms(x_ref[...], g_ref[...]).astype(BF16)
    za_ref[...] = _dot(h, w_ref[:, :A_END])
    zb_ref[...] = _dot(h, w_ref[:, A_END:B_END])
    q = _dot(h, w_ref[:, B_END:Q_END]) * (1.0 / math.sqrt(HEAD_DIM))
    qkv_ref[:, :ATTN_WIDTH] = q.astype(BF16)
    qkv_ref[:, ATTN_WIDTH:] = _dot(h, w_ref[:, Q_END:V_END]).astype(BF16)
    zd_ref[...] = _dot(h, w_ref[:, V_END:])


def _inproj(x2, g, w, tm=512):
    t = x2.shape[0]
    row = lambda i: (i, 0)
    fixed = lambda i: (0, 0)
    return pl.pallas_call(
        _inproj_kernel,
        grid=(t // tm,),
        in_specs=[pl.BlockSpec((tm, D_MODEL), row),
                  pl.BlockSpec((1, D_MODEL), fixed),
                  pl.BlockSpec((D_MODEL, IN_WIDTH), fixed)],
        out_specs=[pl.BlockSpec((tm, A_END), row),
                   pl.BlockSpec((tm, POOL_WIDTH), row),
                   pl.BlockSpec((tm, 3 * ATTN_WIDTH), row),
                   pl.BlockSpec((tm, 2 * CONV_WIDTH), row)],
        out_shape=[jax.ShapeDtypeStruct((t, A_END), F32),
                   jax.ShapeDtypeStruct((t, POOL_WIDTH), F32),
                   jax.ShapeDtypeStruct((t, 3 * ATTN_WIDTH), BF16),
                   jax.ShapeDtypeStruct((t, 2 * CONV_WIDTH), F32)],
        compiler_params=_params(("parallel",)),
        name="inproj",
    )(x2, g, w)


def _gmlp_kernel(z_ref, g_ref, b_ref, w_ref, bias_ref, o_ref, *, chunks):
    gw = GMLP_WIDTH // GMLP_GROUPS
    row = lax.broadcasted_iota(jnp.int32, (GMLP_CHUNK, GMLP_GROUPS * GMLP_CHUNK), 0)
    col = lax.broadcasted_iota(jnp.int32, (GMLP_CHUNK, GMLP_GROUPS * GMLP_CHUNK), 1)
    w = w_ref[...]
    w = jnp.where((col & (GMLP_CHUNK - 1)) <= row, w, jnp.zeros_like(w))
    lane_group = lax.broadcasted_iota(jnp.int32, (GMLP_CHUNK, GMLP_WIDTH), 1) // gw
    for c in range(chunks):
        z = jax.nn.gelu(z_ref[c * GMLP_CHUNK:(c + 1) * GMLP_CHUNK, :])
        u = z[:, :GMLP_WIDTH]
        v = _layernorm(z[:, GMLP_WIDTH:], g_ref[...], b_ref[...]).astype(BF16)
        stacked = jnp.concatenate(
            [jnp.where(lane_group == g, v, jnp.zeros_like(v)) for g in range(GMLP_GROUPS)], axis=0)
        mixed = _dot(w, stacked) + bias_ref[...]
        o_ref[c * GMLP_CHUNK:(c + 1) * GMLP_CHUNK, :] = (u * mixed).astype(BF16)


def _gmlp(za, ln_g, ln_b, w_cat, bias, tm=512):
    t = za.shape[0]
    row = lambda i: (i, 0)
    fixed = lambda i: (0, 0)
    return pl.pallas_call(
        functools.partial(_gmlp_kernel, chunks=tm // GMLP_CHUNK),
        grid=(t // tm,),
        in_specs=[pl.BlockSpec((tm, A_END), row),
                  pl.BlockSpec((1, GMLP_WIDTH), fixed),
                  pl.BlockSpec((1, GMLP_WIDTH), fixed),
                  pl.BlockSpec((GMLP_CHUNK, GMLP_GROUPS * GMLP_CHUNK), fixed),
                  pl.BlockSpec((GMLP_CHUNK, GMLP_WIDTH), fixed)],
        out_specs=pl.BlockSpec((tm, GMLP_WIDTH), row),
        out_shape=jax.ShapeDtypeStruct((t, GMLP_WIDTH), BF16),
        compiler_params=_params(("parallel",)),
        name="gmlp",
    )(za, ln_g, ln_b, w_cat, bias)


def _pool_kernel(z_ref, w_ref, b_ref, s_ref, o_ref):
    x = z_ref[...]
    seq = x.shape[0]
    t = lax.broadcasted_iota(jnp.int32, x.shape, 0)
    lane = lax.broadcasted_iota(jnp.int32, x.shape, 1)

    def lagged(a, k):
        return jnp.where(t >= k, pltpu.roll(a, k, 0), 0.0)

    sums = []
    s = x
    for w in POOL_WINDOWS:
        s = s + lagged(s, w // 2)
        sums.append(s)
    group = lane // POOL_GROUP
    sel = sums[-1]
    width = jnp.full(x.shape, float(POOL_WINDOWS[-1]), F32)
    for g in range(len(POOL_WINDOWS) - 2, -1, -1):
        sel = jnp.where(group == g, sums[g], sel)
        width = jnp.where(group == g, float(POOL_WINDOWS[g]), width)
    count = jnp.minimum((t + 1).astype(F32), width)
    pooled = sel / count - x
    y = _dot(pooled.astype(BF16), w_ref[...]) + b_ref[...]
    o_ref[...] = (y * s_ref[...]).astype(BF16)
    del seq


def _pool(zb, w_bd, b, scale, seq):
    t = zb.shape[0]
    row = lambda i: (i, 0)
    fixed = lambda i: (0, 0)
    return pl.pallas_call(
        _pool_kernel,
        grid=(t // seq,),
        in_specs=[pl.BlockSpec((seq, POOL_WIDTH), row),
                  pl.BlockSpec((POOL_WIDTH, POOL_WIDTH), fixed),
                  pl.BlockSpec((1, POOL_WIDTH), fixed),
                  pl.BlockSpec((1, POOL_WIDTH), fixed)],
        out_specs=pl.BlockSpec((seq, POOL_WIDTH), row),
        out_shape=jax.ShapeDtypeStruct((t, POOL_WIDTH), BF16),
        compiler_params=_params(("parallel",)),
        name="pool",
    )(zb, w_bd, b, scale)


SUBLANES = 8
CONV_PAD = 32
CONV_ROWS = 256
CONV_HALO = CONV_PAD - SUBLANES


def _conv_kernel(z_ref, w_ref, b_ref, g_ref, be_ref, o_ref, pad_ref, sh_ref):
    seq = o_ref.shape[0]
    pad_ref[:CONV_PAD, :] = jnp.zeros((CONV_PAD, CONV_WIDTH), F32)
    pad_ref[CONV_PAD:, :] = z_ref[:, :CONV_WIDTH] * jax.nn.sigmoid(z_ref[:, CONV_WIDTH:])
    span = CONV_ROWS + CONV_HALO
    for r in range(seq // CONV_ROWS):
        r0 = r * CONV_ROWS
        for b in range(SUBLANES):
            sh_ref[b] = pad_ref[r0 + SUBLANES - b:r0 + SUBLANES - b + span, :]
        acc = jnp.zeros((CONV_ROWS, CONV_WIDTH), F32)
        for lag in range(CONV_KERNEL):
            a, b = divmod(lag, SUBLANES)
            start = CONV_HALO - SUBLANES * a
            tap = CONV_KERNEL - 1 - lag
            acc = acc + sh_ref[b, start:start + CONV_ROWS, :] * w_ref[tap:tap + 1, :]
        y = _layernorm(acc + b_ref[...], g_ref[...], be_ref[...])
        o_ref[r0:r0 + CONV_ROWS, :] = jax.nn.silu(y).astype(BF16)


def _conv(zd, w, b, ln_g, ln_b, seq):
    t = zd.shape[0]
    row = lambda i: (i, 0)
    fixed = lambda i: (0, 0)
    return pl.pallas_call(
        _conv_kernel,
        grid=(t // seq,),
        in_specs=[pl.BlockSpec((seq, 2 * CONV_WIDTH), row),
                  pl.BlockSpec((CONV_KERNEL, CONV_WIDTH), fixed),
                  pl.BlockSpec((1, CONV_WIDTH), fixed),
                  pl.BlockSpec((1, CONV_WIDTH), fixed),
                  pl.BlockSpec((1, CONV_WIDTH), fixed)],
        out_specs=pl.BlockSpec((seq, CONV_WIDTH), row),
        out_shape=jax.ShapeDtypeStruct((t, CONV_WIDTH), BF16),
        scratch_shapes=[pltpu.VMEM((seq + CONV_PAD, CONV_WIDTH), F32),
                        pltpu.VMEM((SUBLANES, CONV_ROWS + CONV_HALO, CONV_WIDTH), F32)],
        compiler_params=_params(("parallel",)),
        name="conv",
    )(zd, w, b, ln_g, ln_b)


ATTN_Q_ROWS = 256
HEADS_PER_PAIR = LANES // HEAD_DIM


def _attention_bias(seq):
    qi = jnp.arange(ATTN_Q_ROWS, dtype=jnp.int32)[:, None]
    x = jnp.arange(seq, dtype=jnp.int32)[None, :]
    dist = seq - ATTN_Q_ROWS + qi - x
    mult = jnp.zeros(dist.shape, F32)
    for window, dilation in DILATED_CONFIGS:
        mult = mult + ((dist >= 0) & (dist % dilation == 0) & (dist <= window)).astype(F32)
    slopes = 2.0 ** (-ALIBI_MAX_BIAS * jnp.arange(1, ATTN_HEADS + 1, dtype=F32) / ATTN_HEADS)
    alibi = -slopes[:, None, None] * dist.astype(F32)[None]
    bias = jnp.where(mult > 0, jnp.log(jnp.maximum(mult, 1.0)) + alibi, MASK_VALUE)
    return bias.reshape(ATTN_HEADS * ATTN_Q_ROWS, seq)


def _attn_kernel(q_ref, k_ref, v_ref, bias_ref, o_ref):
    seq = q_ref.shape[0]
    lane = lax.broadcasted_iota(jnp.int32, (ATTN_Q_ROWS, LANES), 1)
    first_head = lane < HEAD_DIM
    nt = (((1,), (1,)), ((), ()))
    for i in range(seq // ATTN_Q_ROWS):
        rows = slice(i * ATTN_Q_ROWS, (i + 1) * ATTN_Q_ROWS)
        n_keys = (i + 1) * ATTN_Q_ROWS
        q = q_ref[rows, :]
        zero = jnp.zeros_like(q)
        stacked = jnp.concatenate([jnp.where(first_head, q, zero), jnp.where(first_head, zero, q)], axis=0)
        s = lax.dot_general(stacked, k_ref[:n_keys, :], nt, preferred_element_type=F32)
        s = s + bias_ref[:, seq - n_keys:]
        p = jnp.exp(s - jnp.max(s, axis=-1, keepdims=True))
        den = jnp.sum(p, axis=-1, keepdims=True)
        o = _dot(p.astype(BF16), v_ref[:n_keys, :]) / den
        o_ref[rows, :] = jnp.where(first_head, o[:ATTN_Q_ROWS], o[ATTN_Q_ROWS:]).astype(BF16)


def _attention(qkv, bias, seq):
    t = qkv.shape[0]
    pairs = ATTN_WIDTH // LANES
    return pl.pallas_call(
        _attn_kernel,
        grid=(pairs, t // seq),
        in_specs=[pl.BlockSpec((seq, LANES), lambda p, b: (b, p)),
                  pl.BlockSpec((seq, LANES), lambda p, b: (b, pairs + p)),
                  pl.BlockSpec((seq, LANES), lambda p, b: (b, 2 * pairs + p)),
                  pl.BlockSpec((HEADS_PER_PAIR * ATTN_Q_ROWS, seq), lambda p, b: (p, 0))],
        out_specs=pl.BlockSpec((seq, LANES), lambda p, b: (b, p)),
        out_shape=jax.ShapeDtypeStruct((t, ATTN_WIDTH), BF16),
        compiler_params=_params(("parallel", "parallel")),
        name="attention",
    )(qkv, qkv, qkv, bias)


TOP_K = 2


def _route(h, w_router):
    h_hi = h.astype(BF16)
    h_lo = (h - h_hi.astype(F32)).astype(BF16)
    w_hi = w_router.astype(BF16)
    w_lo = (w_router - w_hi.astype(F32)).astype(BF16)
    hi = _dot(h_hi, jnp.concatenate([w_hi, w_lo], axis=1))
    logits = hi[:, :LANES] + hi[:, LANES:] + _dot(h_lo, w_hi)
    lane = lax.broadcasted_iota(jnp.int32, logits.shape, 1)
    neg = jnp.float32(-jnp.inf)
    logits = jnp.where(lane < N_EXPERTS, logits, neg)
    m1 = jnp.max(logits, axis=-1, keepdims=True)
    i1 = jnp.min(jnp.where(logits == m1, lane, LANES), axis=-1, keepdims=True)
    rest = jnp.where(lane == i1, neg, logits)
    m2 = jnp.max(rest, axis=-1, keepdims=True)
    i2 = jnp.min(jnp.where(rest == m2, lane, LANES), axis=-1, keepdims=True)
    e2 = jnp.exp(m2 - m1)
    den = 1.0 + e2
    return jnp.where(lane == 0, i1.astype(F32),
                     jnp.where(lane == 1, i2.astype(F32),
                               jnp.where(lane == 2, 1.0 / den, jnp.where(lane == 3, e2 / den, 0.0))))


def _merge_kernel(x_ref, g_ref, ya_ref, yb_ref, yc_ref, yd_ref, wg_ref, bg_ref,
                  wa_ref, wb_ref, wc_ref, wd_ref, wo_ref, *rest, with_router):
    x = x_ref[...]
    h = _rms(x, g_ref[...]).astype(BF16)
    merged = None
    for i, (y_ref, w_ref) in enumerate(((ya_ref, wa_ref), (yb_ref, wb_ref), (yc_ref, wc_ref), (yd_ref, wd_ref))):
        cols = slice(i * D_MODEL, (i + 1) * D_MODEL)
        gate = jax.nn.sigmoid(_dot(h, wg_ref[:, cols]) + bg_ref[:, cols])
        term = gate * _dot(y_ref[...], w_ref[...])
        merged = term if merged is None else merged + term
    x_new = x + _dot(merged.astype(BF16), wo_ref[...])
    if with_router:
        gf_ref, wr_ref, o_ref, r_ref = rest
        r_ref[...] = _route(_rms(x_new, gf_ref[...]), wr_ref[...])
    else:
        (o_ref,) = rest
    o_ref[...] = x_new


def _merge(x2, g, ya, yb, yc, yd, wg, bg, wa, wb, wc, wd, wo, router=None, tm=512):
    t = x2.shape[0]
    row = lambda i: (i, 0)
    fixed = lambda i: (0, 0)
    full = lambda a: pl.BlockSpec(a.shape, fixed)
    extra = () if router is None else tuple(router)
    out_specs = [pl.BlockSpec((tm, D_MODEL), row)]
    out_shape = [jax.ShapeDtypeStruct((t, D_MODEL), F32)]
    if router is not None:
        out_specs.append(pl.BlockSpec((tm, LANES), row))
        out_shape.append(jax.ShapeDtypeStruct((t, LANES), F32))
    return pl.pallas_call(
        functools.partial(_merge_kernel, with_router=router is not None),
        grid=(t // tm,),
        in_specs=[pl.BlockSpec((tm, D_MODEL), row), full(g),
                  pl.BlockSpec((tm, GMLP_WIDTH), row), pl.BlockSpec((tm, POOL_WIDTH), row),
                  pl.BlockSpec((tm, ATTN_WIDTH), row), pl.BlockSpec((tm, CONV_WIDTH), row),
                  full(wg), full(bg), full(wa), full(wb), full(wc), full(wd), full(wo)]
                 + [full(a) for a in extra],
        out_specs=out_specs,
        out_shape=out_shape,
        compiler_params=_params(("parallel",)),
        name="merge",
    )(x2, g, ya, yb, yc, yd, wg, bg, wa, wb, wc, wd, wo, *extra)


FF_CHUNK = 256


def _swiglu_hidden(h, w1_ref, w3_ref, act_ref):
    for c in range(act_ref.shape[1] // FF_CHUNK):
        cols = slice(c * FF_CHUNK, (c + 1) * FF_CHUNK)
        act_ref[:, cols] = (jax.nn.silu(_dot(h, w1_ref[:, cols])) * _dot(h, w3_ref[:, cols])).astype(BF16)


def _dense_ffn_kernel(x_ref, g_ref, w1_ref, w3_ref, w2_ref, o_ref, act_ref):
    x = x_ref[...]
    _swiglu_hidden(_rms(x, g_ref[...]).astype(BF16), w1_ref, w3_ref, act_ref)
    o_ref[...] = x + _dot(act_ref[...], w2_ref[...])


def _resident(shape):
    return pl.BlockSpec(shape, lambda *_: (0,) * len(shape), pipeline_mode=pl.Buffered(1))


def _dense_ffn(x2, g, w1, w3, w2, tm=1024):
    t = x2.shape[0]
    d_ff = w1.shape[1]
    assert d_ff % FF_CHUNK == 0
    return pl.pallas_call(
        _dense_ffn_kernel,
        grid=(t // tm,),
        in_specs=[pl.BlockSpec((tm, D_MODEL), lambda i: (i, 0)),
                  _resident((1, D_MODEL)), _resident(w1.shape), _resident(w3.shape), _resident(w2.shape)],
        out_specs=pl.BlockSpec((tm, D_MODEL), lambda i: (i, 0)),
        out_shape=jax.ShapeDtypeStruct((t, D_MODEL), F32),
        scratch_shapes=[pltpu.VMEM((tm, d_ff), BF16)],
        compiler_params=_params(("parallel",)),
        name="dense_ffn",
    )(x2, g, w1, w3, w2)


MOE_ROWS = 768
MOE_FF_TILE = 512
MOE_DMA_STEPS = 6
MOE_DMA_CHUNK = MOE_ROWS // MOE_DMA_STEPS


def _moe_plan(routed, n_tiles):
    t = routed.shape[0]
    n = TOP_K * t
    expert = routed[:, :TOP_K].astype(jnp.int32).reshape(n)
    assignment = jnp.arange(n, dtype=jnp.int32)
    order = jnp.sort(expert * n + assignment) % n
    experts = jnp.arange(N_EXPERTS, dtype=jnp.int32)
    counts = jnp.sum((expert[:, None] == experts[None, :]).astype(jnp.int32), axis=0)
    first_sorted = jnp.cumsum(counts) - counts
    tiles_per = (counts + MOE_ROWS - 1) // MOE_ROWS
    tile_start = jnp.cumsum(tiles_per) - tiles_per
    tile = jnp.arange(n_tiles, dtype=jnp.int32)
    tile_expert = jnp.sum((tile[:, None] >= tile_start[None, :]).astype(jnp.int32), axis=1) - 1
    tile_valid = (tile < jnp.sum(tiles_per)).astype(jnp.int32)
    n_slots = n_tiles * MOE_ROWS
    slot = jnp.arange(n_slots, dtype=jnp.int32)
    slot_expert = jnp.repeat(tile_expert, MOE_ROWS)
    owner = slot_expert[:, None] == experts[None, :]
    pick = lambda v: jnp.sum(jnp.where(owner, v[None, :], 0), axis=1)
    rank = slot - pick(tile_start) * MOE_ROWS
    used = (rank < pick(counts)) & (jnp.repeat(tile_valid, MOE_ROWS) > 0)
    assign = order[jnp.clip(pick(first_sorted) + rank, 0, n - 1)]
    token = assign // TOP_K
    src = jnp.where(used, token, 0)
    dump = n + MOE_ROWS - 1 + jnp.cumsum(jnp.logical_not(used).astype(jnp.int32))
    dest = jnp.where(used, (assign % TOP_K) * t + token, dump)
    zeros = jnp.zeros((1, MOE_ROWS), jnp.int32)
    src_tiles = src.reshape(n_tiles, MOE_ROWS)
    src_next = jnp.concatenate([src_tiles[1:], zeros, zeros], axis=0)
    first_dump = (n + jnp.arange(MOE_ROWS, dtype=jnp.int32)).reshape(1, MOE_ROWS)
    dest_prev = jnp.concatenate([first_dump, dest.reshape(n_tiles, MOE_ROWS)], axis=0)
    tile_expert = jnp.concatenate([tile_expert, tile_expert[-1:]])
    tile_valid = jnp.concatenate([tile_valid, jnp.zeros((1,), jnp.int32)])
    return (tile_expert, tile_valid, src_tiles[0], src_next.reshape(n_tiles + 1, 1, MOE_ROWS),
            dest_prev.reshape(n_tiles + 1, 1, MOE_ROWS))


def _experts_kernel(te_ref, tv_ref, src0_ref, srcn_ref, dstp_ref, x_hbm, g_ref, w1_ref, w3_ref, w2_ref,
                    y_hbm, xbuf, hb_ref, act_ref, obuf, sem_g, sem_s):
    j = pl.program_id(0)
    n_f = w1_ref.shape[0]
    valid = tv_ref[j] > 0
    out_slot = j % 2
    prev_slot = 1 - out_slot

    def gather_row(tok, r):
        return pltpu.make_async_copy(x_hbm.at[pl.ds(tok, 1), :], xbuf.at[pl.ds(r, 1), :], sem_g)

    def scatter_row(slot, r, d):
        return pltpu.make_async_copy(obuf.at[slot, pl.ds(r, 1), :], y_hbm.at[pl.ds(d, 1), :], sem_s.at[slot])

    def wait_gather():
        pltpu.make_async_copy(x_hbm.at[pl.ds(0, MOE_ROWS), :], xbuf, sem_g).wait()

    def wait_scatter(slot):
        pltpu.make_async_copy(obuf.at[slot], y_hbm.at[pl.ds(0, MOE_ROWS), :], sem_s.at[slot]).wait()

    @pl.when(j == 0)
    def _():
        obuf[...] = jnp.zeros_like(obuf)

        def first(r, c):
            gather_row(src0_ref[r], r).start()
            return c
        lax.fori_loop(0, MOE_ROWS, first, 0)

    wait_gather()
    hb_ref[...] = _rms(xbuf[...], g_ref[...]).astype(BF16)

    def row_dmas(f):
        base = pl.multiple_of(f * MOE_DMA_CHUNK, MOE_DMA_CHUNK)
        for c in range(MOE_DMA_CHUNK):
            r = base + c
            gather_row(srcn_ref[0, 0, r], r).start()
            scatter_row(prev_slot, r, dstp_ref[0, 0, r]).start()

    def hidden_chunk(f):
        h = hb_ref[...]
        act_ref[f] = (jax.nn.silu(_dot(h, w1_ref[f])) * _dot(h, w3_ref[f])).astype(BF16)

    @pl.when(valid)
    def _():
        def with_dmas(f, c):
            row_dmas(f)
            hidden_chunk(f)
            return c
        lax.fori_loop(0, MOE_DMA_STEPS, with_dmas, 0)
        for f in range(MOE_DMA_STEPS, n_f):
            hidden_chunk(f)
        y = _dot(act_ref[0], w2_ref[0])
        for f in range(1, n_f):
            y = y + _dot(act_ref[f], w2_ref[f])

        @pl.when(j >= 1)
        def _():
            wait_scatter(out_slot)
        obuf[out_slot] = y

    @pl.when(jnp.logical_not(valid))
    def _():
        def only_dmas(f, c):
            row_dmas(f)
            return c
        lax.fori_loop(0, MOE_DMA_STEPS, only_dmas, 0)
        wait_scatter(out_slot)

    @pl.when(j == pl.num_programs(0) - 1)
    def _():
        wait_scatter(prev_slot)
        wait_gather()


def _experts(x2, g, plan, w1, w3, w2, n_tiles):
    tile_expert, tile_valid, src0, src_next, dest_prev = plan
    t = x2.shape[0]
    n_exp, n_f = w1.shape[:2]
    assert n_f > MOE_DMA_STEPS and MOE_ROWS % MOE_DMA_STEPS == 0
    n_rows = (n_tiles + 1) * MOE_ROWS
    assert n_rows >= TOP_K * t + MOE_ROWS
    per_tile = lambda j, te, tv, s0: (j, 0, 0)
    expert = lambda j, te, tv, s0: (te[j], 0, 0, 0)

    grid_spec = pltpu.PrefetchScalarGridSpec(
        num_scalar_prefetch=3,
        grid=(n_tiles + 1,),
        in_specs=[pl.BlockSpec((1, 1, MOE_ROWS), per_tile, memory_space=pltpu.SMEM),
                  pl.BlockSpec((1, 1, MOE_ROWS), per_tile, memory_space=pltpu.SMEM),
                  pl.BlockSpec(memory_space=pl.ANY),
                  pl.BlockSpec((1, D_MODEL), lambda j, te, tv, s0: (0, 0)),
                  pl.BlockSpec((None,) + w1.shape[1:], expert, pipeline_mode=pl.Buffered(1)),
                  pl.BlockSpec((None,) + w3.shape[1:], expert, pipeline_mode=pl.Buffered(1)),
                  pl.BlockSpec((None,) + w2.shape[1:], expert, pipeline_mode=pl.Buffered(1))],
        out_specs=pl.BlockSpec(memory_space=pl.ANY),
        scratch_shapes=[pltpu.VMEM((MOE_ROWS, D_MODEL), F32),
                        pltpu.VMEM((MOE_ROWS, D_MODEL), BF16),
                        pltpu.VMEM((n_f, MOE_ROWS, MOE_FF_TILE), BF16),
                        pltpu.VMEM((2, MOE_ROWS, D_MODEL), F32),
                        pltpu.SemaphoreType.DMA(()),
                        pltpu.SemaphoreType.DMA((2,))],
    )
    return pl.pallas_call(
        _experts_kernel,
        grid_spec=grid_spec,
        out_shape=jax.ShapeDtypeStruct((n_rows, D_MODEL), F32),
        compiler_params=pltpu.CompilerParams(dimension_semantics=("arbitrary",), vmem_limit_bytes=VMEM_LIMIT),
        name="experts",
    )(tile_expert, tile_valid, src0, src_next, dest_prev, x2, g, w1, w3, w2)


def _combine_kernel(x_ref, r_ref, y0_ref, y1_ref, g_ref, o_ref):
    routed = r_ref[...]
    w0 = routed[:, TOP_K:TOP_K + 1]
    w1 = routed[:, TOP_K + 1:TOP_K + 2]
    o_ref[...] = _rms(x_ref[...] + w0 * y0_ref[...] + w1 * y1_ref[...], g_ref[...])


def _combine(x2, routed, y_rows, g_final, tm=512):
    t = x2.shape[0]
    second = t // tm
    return pl.pallas_call(
        _combine_kernel,
        grid=(t // tm,),
        in_specs=[pl.BlockSpec((tm, D_MODEL), lambda i: (i, 0)),
                  pl.BlockSpec((tm, LANES), lambda i: (i, 0)),
                  pl.BlockSpec((tm, D_MODEL), lambda i: (i, 0)),
                  pl.BlockSpec((tm, D_MODEL), lambda i: (second + i, 0)),
                  pl.BlockSpec((1, D_MODEL), lambda i: (0, 0))],
        out_specs=pl.BlockSpec((tm, D_MODEL), lambda i: (i, 0)),
        out_shape=jax.ShapeDtypeStruct((t, D_MODEL), F32),
        compiler_params=_params(("parallel",)),
        name="combine",
    )(x2, routed, y_rows, y_rows, g_final)


def _moe_ffn(x2, routed, g, w1, w3, w2, g_final):
    t = x2.shape[0]
    n_tiles = (TOP_K * t + N_EXPERTS * (MOE_ROWS - 1)) // MOE_ROWS
    plan = _moe_plan(routed, n_tiles)
    y_rows = _experts(x2, g, plan, w1, w3, w2, n_tiles)
    return _combine(x2, routed, y_rows, g_final)


def kernel(x, norm_mix, w_in, gmlp_ln_g, gmlp_ln_b, gmlp_w_s, gmlp_b_s, pool_w, pool_b, pool_scale, conv_w, conv_b, conv_ln_g, conv_ln_b, w_br_a, w_br_b, w_br_c, w_br_d, w_gate, b_gate, w_out, norm_ffn, dense_w1, dense_w3, dense_w2, moe_router, moe_w1, moe_w3, moe_w2, norm_final):
    b, s, d = x.shape
    depth = norm_mix.shape[0]
    assert depth == 2 and d == D_MODEL, "kernel is written for the two-layer (dense, MoE) trunk"
    x2 = x.reshape(b * s, d)
    attn_bias = _attention_bias(s)
    row = lambda a: a.reshape(1, -1)

    for l in range(depth):
        za, zb, qkv, zd = _inproj(x2, row(norm_mix[l]), w_in[l].astype(BF16))

        w_cat = gmlp_w_s[l].transpose(1, 0, 2).reshape(GMLP_CHUNK, GMLP_GROUPS * GMLP_CHUNK).astype(BF16)
        bias_a = jnp.repeat(gmlp_b_s[l].T, GMLP_WIDTH // GMLP_GROUPS, axis=1)
        ya = _gmlp(za, row(gmlp_ln_g[l]), row(gmlp_ln_b[l]), w_cat, bias_a)

        w_bd = jax.scipy.linalg.block_diag(*[pool_w[l, g] for g in range(len(POOL_WINDOWS))]).astype(BF16)
        yb = _pool(zb, w_bd, row(pool_b[l]), row(pool_scale[l]), s)

        yc = _attention(qkv, attn_bias, s)
        yd = _conv(zd, conv_w[l], row(conv_b[l]), row(conv_ln_g[l]), row(conv_ln_b[l]), s)

        i = l // 2
        router = None
        if l % 2 == 1:
            router = (row(norm_ffn[l]), jnp.pad(moe_router[i], ((0, 0), (0, LANES - N_EXPERTS))))
        merged = _merge(x2, row(norm_mix[l]), ya, yb, yc, yd, w_gate[l].astype(BF16), row(b_gate[l]),
                        w_br_a[l].astype(BF16), w_br_b[l].astype(BF16), w_br_c[l].astype(BF16),
                        w_br_d[l].astype(BF16), w_out[l].astype(BF16), router=router)
        if l % 2 == 0:
            (x2,) = merged
            x2 = _dense_ffn(x2, row(norm_ffn[l]), dense_w1[i].astype(BF16), dense_w3[i].astype(BF16),
                            dense_w2[i].astype(BF16))
        else:
            x2, routed = merged
            n_exp, _, d_ff = moe_w1[i].shape
            chunks = d_ff // MOE_FF_TILE
            up = lambda w: w.astype(BF16).reshape(n_exp, d, chunks, MOE_FF_TILE).transpose(0, 2, 1, 3)
            down = moe_w2[i].astype(BF16).reshape(n_exp, chunks, MOE_FF_TILE, d)
            x2 = _moe_ffn(x2, routed, row(norm_ffn[l]), up(moe_w1[i]), up(moe_w3[i]), down, row(norm_final))
    return x2.reshape(b, s, d)
```
